```python
import math
import jax, jax.numpy as jnp
from jax import lax
import numpy as np

D_MODEL = 1024
BATCH = 32
SEQ = 2048
DEPTH = 4
DEC_BATCH = 8
DEC_SEQ = 2048
PAST_LEN = 128

N_HEADS = 16
N_KV_HEADS = 4
HEAD_DIM = D_MODEL // N_HEADS
Q_BLOCK = 128
GRID_W = 64
ROPE_THETA = 10000.0
QK_EPS = 1e-6
SHORT_CONV = 3
FILTER_EMB = 33
FILTER_HIDDEN = 64
DECAY_TARGET = 1e-2
FAST_DECAY_PCT = 0.3
SLOW_DECAY_PCT = 1.5
N_MIXERS = 2
N_ATTN_LAYERS = (DEPTH + 1) // 2
N_HYENA_LAYERS = DEPTH // 2
N_EXPERTS = 16
EC_CAPACITY = 2
EXPERT_FF = 2816
DN_ALPHA = (2 * DEPTH) ** 0.25
DN_BETA = (8 * DEPTH) ** -0.25
LN_EPS = 1e-5

kernel_name = "hybrid_axial_gqa_hyena_ec_moe_encoder"

F32 = jnp.float32


def layer_norm(x, g, b):
    xf = x.astype(F32)
    mu = jnp.mean(xf, -1, keepdims=True)
    var = jnp.mean(jnp.square(xf - mu), -1, keepdims=True)
    return ((xf - mu) * lax.rsqrt(var + LN_EPS) * g.astype(F32) + b.astype(F32)).astype(x.dtype)


def rms_norm(x, g):
    xf = x.astype(F32)
    return (xf * lax.rsqrt(jnp.mean(xf * xf, -1, keepdims=True) + QK_EPS) * g.astype(F32)).astype(x.dtype)


def axial_rope_tables(L):
    rows = L // GRID_W
    row = jnp.repeat(jnp.arange(rows), GRID_W)
    col = jnp.tile(jnp.arange(GRID_W), rows)
    axis_dim = HEAD_DIM // 2
    inv = ROPE_THETA ** (-jnp.arange(0, axis_dim, 2, dtype=F32) / axis_dim)
    ang = jnp.concatenate([row[:, None] * inv, col[:, None] * inv], -1)
    return jnp.cos(ang), jnp.sin(ang)


def apply_rope(x, cos, sin):
    xf = x.astype(F32)
    x1, x2 = jnp.split(xf, 2, axis=-1)
    c, s = cos[None, :, None, :], sin[None, :, None, :]
    return jnp.concatenate([x1 * c - x2 * s, x1 * s + x2 * c], -1).astype(x.dtype)


def axial_gqa_attention(x, w_qkv, q_gain, k_gain, w_o):
    B, L, _ = x.shape
    G = N_HEADS // N_KV_HEADS
    qkv = x @ w_qkv
    q, k, v = jnp.split(qkv, [N_HEADS * HEAD_DIM, (N_HEADS + N_KV_HEADS) * HEAD_DIM], axis=-1)
    q = q.reshape(B, L, N_HEADS, HEAD_DIM)
    k = k.reshape(B, L, N_KV_HEADS, HEAD_DIM)
    v = v.reshape(B, L, N_KV_HEADS, HEAD_DIM)
    cos, sin = axial_rope_tables(L)
    q = apply_rope(rms_norm(q, q_gain), cos, sin) * (HEAD_DIM ** -0.5)
    k = apply_rope(rms_norm(k, k_gain), cos, sin)
    nb = L // Q_BLOCK
    qb = q.reshape(B, nb, Q_BLOCK, N_KV_HEADS, G, HEAD_DIM).transpose(1, 0, 2, 3, 4, 5)

    def attend(q_blk):
        s = jnp.einsum('bqkgd,bskd->bkgqs', q_blk, k, preferred_element_type=F32)
        p = jax.nn.softmax(s, axis=-1).astype(v.dtype)
        return jnp.einsum('bkgqs,bskd->bqkgd', p, v)

    o = lax.map(attend, qb)
    o = o.transpose(1, 0, 2, 3, 4, 5).reshape(B, L, N_HEADS * HEAD_DIM)
    return o @ w_o


def hyena_two_sided_filter(L, w1, b1, freq, w2, b2, w3):
    t = jnp.linspace(0.0, 1.0, L, dtype=F32)[:, None]
    bands = (FILTER_EMB - 1) // 2
    w = 2.0 * math.pi * jnp.arange(L, dtype=F32) / L
    f = jnp.linspace(1e-4, bands - 1, bands, dtype=F32)
    fw = w[:, None] * f[None, :]
    emb = jnp.concatenate([t, jnp.cos(fw), -jnp.sin(fw)], -1)
    fq = freq.astype(F32)
    h = jnp.sin(fq * (emb @ w1.astype(F32) + b1.astype(F32)))
    h = jnp.sin(fq * (h @ w2.astype(F32) + b2.astype(F32)))
    h = (h @ w3.astype(F32)).reshape(L, 2, D_MODEL)
    min_decay = math.log(DECAY_TARGET) / SLOW_DECAY_PCT
    max_decay = math.log(DECAY_TARGET) / FAST_DECAY_PCT
    deltas = jnp.abs(jnp.linspace(min_decay, max_decay, D_MODEL, dtype=F32))
    h = h * jnp.exp(-t * deltas[None, :])[:, None, :]
    return jnp.concatenate([h[:, 0], jnp.zeros((1, D_MODEL), F32), h[:0:-1, 1]], 0)


def hyena_operator(x, w_in, b_in, conv_w, conv_b, f_w1, f_b1, f_freq, f_w2, f_b2, f_w3,
                   skip, w_out, b_out):
    B, L, D = x.shape
    u = x @ w_in + b_in
    up = jnp.pad(u, ((0, 0), (SHORT_CONV // 2, SHORT_CONV // 2), (0, 0)))
    u = sum(up[:, j:j + L] * conv_w[j] for j in range(SHORT_CONV)) + conv_b
    x0, x1, v = jnp.split(u, 3, axis=-1)
    z = (v * x1).astype(F32)
    kern = hyena_two_sided_filter(L, f_w1, f_b1, f_freq, f_w2, f_b2, f_w3)
    n_fft = 2 * L
    y = jnp.fft.irfft(jnp.fft.rfft(z, n=n_fft, axis=1) * jnp.fft.rfft(kern, n=n_fft, axis=0)[None],
                      n=n_fft, axis=1)[:, :L]
    y = (y + z * skip.astype(F32)).astype(x.dtype) * x0
    return y @ w_out + b_out


def expert_choice_moe(x, w_router, w_gate, w_up, w_down):
    B, L, D = x.shape
    n = B * L
    cap = EC_CAPACITY * n // N_EXPERTS
    xt = x.reshape(n, D)
    aff = jax.nn.softmax((xt @ w_router).astype(F32), axis=-1)
    gate, idx = lax.top_k(aff.T, cap)

    def expert(args):
        ids, g, wg, wu, wd = args
        xe = xt[ids]
        h = jax.nn.silu(xe @ wg) * (xe @ wu)
        return ((h @ wd).astype(F32) * g[:, None]).astype(x.dtype)

    ye = lax.map(expert, (idx, gate, w_gate, w_up, w_down))
    y = jnp.zeros((n, D), x.dtype).at[idx.reshape(-1)].add(ye.reshape(-1, D))
    return y.reshape(B, L, D)


def trunk(x, attn_w_qkv, attn_q_gain, attn_k_gain, attn_w_o,
          hy_w_in, hy_b_in, hy_conv_w, hy_conv_b, hy_f_w1, hy_f_b1, hy_f_freq, hy_f_w2, hy_f_b2,
          hy_f_w3, hy_skip, hy_w_out, hy_b_out, ln_mix_g, ln_mix_b,
          moe_router, moe_w_gate, moe_w_up, moe_w_down, ln_ffn_g, ln_ffn_b):
    for i in range(DEPTH):
        j = i // N_MIXERS
        if i % N_MIXERS == 0:
            mix = axial_gqa_attention(x, attn_w_qkv[j], attn_q_gain[j], attn_k_gain[j], attn_w_o[j])
        else:
            mix = hyena_operator(x, hy_w_in[j], hy_b_in[j], hy_conv_w[j], hy_conv_b[j],
                                 hy_f_w1[j], hy_f_b1[j], hy_f_freq[j], hy_f_w2[j], hy_f_b2[j],
                                 hy_f_w3[j], hy_skip[j], hy_w_out[j], hy_b_out[j])
        x = layer_norm(DN_ALPHA * x + mix, ln_mix_g[i], ln_mix_b[i])
        ffn = expert_choice_moe(x, moe_router[i], moe_w_gate[i], moe_w_up[i], moe_w_down[i])
        x = layer_norm(DN_ALPHA * x + ffn, ln_ffn_g[i], ln_ffn_b[i])
    return x


def setup_inputs(seed: int = 0) -> dict:
    key = jax.random.key(seed)
    ks = jax.random.split(key, 32)

    def nrm(k, shape, s):
        return jax.random.normal(k, shape, F32) * s

    D = D_MODEL
    qkv_cols = (N_HEADS + 2 * N_KV_HEADS) * HEAD_DIM
    qkv_scale = jnp.concatenate([jnp.ones(((N_HEADS + N_KV_HEADS) * HEAD_DIM,), F32),
                                 jnp.full((N_KV_HEADS * HEAD_DIM,), DN_BETA, F32)])
    hy_scale = jnp.concatenate([jnp.ones((2 * D,), F32), jnp.full((D,), DN_BETA, F32)])
    return {
        "x_prompt": nrm(ks[0], (BATCH, SEQ, D), 1.0),
        "x_sample": nrm(ks[1], (DEC_BATCH, DEC_SEQ, D), 1.0),
        "attn_w_qkv": nrm(ks[2], (N_ATTN_LAYERS, D, qkv_cols), D ** -0.5) * qkv_scale,
        "attn_q_gain": 1.0 + nrm(ks[3], (N_ATTN_LAYERS, HEAD_DIM), 0.02),
        "attn_k_gain": 1.0 + nrm(ks[4], (N_ATTN_LAYERS, HEAD_DIM), 0.02),
        "attn_w_o": nrm(ks[5], (N_ATTN_LAYERS, N_HEADS * HEAD_DIM, D), (N_HEADS * HEAD_DIM) ** -0.5 * DN_BETA),
        "hy_w_in": nrm(ks[6], (N_HYENA_LAYERS, D, 3 * D), D ** -0.5) * hy_scale,
        "hy_b_in": nrm(ks[7], (N_HYENA_LAYERS, 3 * D), 0.02),
        "hy_conv_w": nrm(ks[8], (N_HYENA_LAYERS, SHORT_CONV, 3 * D), SHORT_CONV ** -0.5),
        "hy_conv_b": nrm(ks[9], (N_HYENA_LAYERS, 3 * D), 0.02),
        "hy_f_w1": nrm(ks[10], (N_HYENA_LAYERS, FILTER_EMB, FILTER_HIDDEN), FILTER_EMB ** -0.5),
        "hy_f_b1": nrm(ks[11], (N_HYENA_LAYERS, FILTER_HIDDEN), 0.02),
        "hy_f_freq": 1.0 + nrm(ks[12], (N_HYENA_LAYERS, FILTER_HIDDEN), 0.02),
        "hy_f_w2": nrm(ks[13], (N_HYENA_LAYERS, FILTER_HIDDEN, FILTER_HIDDEN), FILTER_HIDDEN ** -0.5),
        "hy_f_b2": nrm(ks[14], (N_HYENA_LAYERS, FILTER_HIDDEN), 0.02),
        "hy_f_w3": nrm(ks[15], (N_HYENA_LAYERS, FILTER_HIDDEN, 2 * D), FILTER_HIDDEN ** -0.5),
        "hy_skip": nrm(ks[16], (N_HYENA_LAYERS, D), 1.0),
        "hy_w_out": nrm(ks[17], (N_HYENA_LAYERS, D, D), D ** -0.5 * DN_BETA),
        "hy_b_out": nrm(ks[18], (N_HYENA_LAYERS, D), 0.02),
        "ln_mix_g": 1.0 + nrm(ks[19], (DEPTH, D), 0.02),
        "ln_mix_b": nrm(ks[20], (DEPTH, D), 0.02),
        "moe_router": nrm(ks[21], (DEPTH, D, N_EXPERTS), D ** -0.5),
        "moe_w_gate": nrm(ks[22], (DEPTH, N_EXPERTS, D, EXPERT_FF), D ** -0.5),
        "moe_w_up": nrm(ks[23], (DEPTH, N_EXPERTS, D, EXPERT_FF), D ** -0.5),
        "moe_w_down": nrm(ks[24], (DEPTH, N_EXPERTS, EXPERT_FF, D), EXPERT_FF ** -0.5 * DN_BETA),
        "ln_ffn_g": 1.0 + nrm(ks[25], (DEPTH, D), 0.02),
        "ln_ffn_b": nrm(ks[26], (DEPTH, D), 0.02),
    }


def reference(x_prompt, x_sample, attn_w_qkv, attn_q_gain, attn_k_gain, attn_w_o,
              hy_w_in, hy_b_in, hy_conv_w, hy_conv_b, hy_f_w1, hy_f_b1, hy_f_freq, hy_f_w2,
              hy_f_b2, hy_f_w3, hy_skip, hy_w_out, hy_b_out, ln_mix_g, ln_mix_b,
              moe_router, moe_w_gate, moe_w_up, moe_w_down, ln_ffn_g, ln_ffn_b):
    weights = (attn_w_qkv, attn_q_gain, attn_k_gain, attn_w_o,
               hy_w_in, hy_b_in, hy_conv_w, hy_conv_b, hy_f_w1, hy_f_b1, hy_f_freq, hy_f_w2,
               hy_f_b2, hy_f_w3, hy_skip, hy_w_out, hy_b_out, ln_mix_g, ln_mix_b,
               moe_router, moe_w_gate, moe_w_up, moe_w_down, ln_ffn_g, ln_ffn_b)
    y_prompt = trunk(x_prompt, *weights)
    y_sample = trunk(x_sample, *weights)
    return (y_prompt, y_sample)
```

```python
import functools
import math

import jax
import jax.numpy as jnp
from jax import lax
from jax.experimental import pallas as pl
from jax.experimental.pallas import tpu as pltpu

F32 = jnp.float32
BF16 = jnp.bfloat16
I32 = jnp.int32

N_HEADS = 16
N_KV_HEADS = 4
GRID_W = 64
ROPE_THETA = 10000.0
QK_EPS = 1e-6
FILTER_EMB = 33
DECAY_TARGET = 1e-2
FAST_DECAY_PCT = 0.3
SLOW_DECAY_PCT = 1.5
EC_CAPACITY = 2
LN_EPS = 1e-5

LANES = 128
MXU_DIM = 256
VMEM_LIMIT_BYTES = 56 * 1024 * 1024

ROW_TILE = 256
TOK_TILE = 256
SLOT_BLOCK = 32
N_WIN = 3
CONV_BLOCK = 256
CONV_CH = 8
HY_CH = 128


def _params(sem, vmem=VMEM_LIMIT_BYTES):
    return pltpu.CompilerParams(dimension_semantics=sem, vmem_limit_bytes=vmem)


def _layer_norm(v, g, b):
    mu = jnp.mean(v, axis=-1, keepdims=True)
    c = v - mu
    var = jnp.mean(c * c, axis=-1, keepdims=True)
    return c * lax.rsqrt(var + LN_EPS) * g + b


def _split_bf16(v):
    hi = v.astype(BF16)
    lo = (v - hi.astype(F32)).astype(BF16)
    return hi, lo


def _qkv_kernel(x_ref, w_ref, cc_ref, ss_ref, gq_ref, gk_ref, seg_ref, q_ref, k_ref, v_ref, *,
                n_q_chunks, n_k_chunks, head_dim, scale):
    xb = x_ref[...].astype(BF16)
    qkv = jnp.dot(xb, w_ref[...], preferred_element_type=F32)
    cc = cc_ref[...]
    ss = ss_ref[...]
    seg = seg_ref[...]
    lane = lax.broadcasted_iota(I32, (xb.shape[0], LANES), 1)
    first_half = (lane % head_dim) < (head_dim // 2)
    half = head_dim // 2
    for c in range(n_q_chunks + n_k_chunks):
        u = qkv[:, c * LANES:(c + 1) * LANES]
        s_hi, s_lo = _split_bf16(u * u)
        ssum = (jnp.dot(s_hi, seg, preferred_element_type=F32)
                + jnp.dot(s_lo, seg, preferred_element_type=F32))
        r = lax.rsqrt(ssum * (1.0 / head_dim) + QK_EPS)
        gain = gq_ref[...] if c < n_q_chunks else gk_ref[...]
        un = u * r * gain
        partner = jnp.where(first_half, pltpu.roll(un, LANES - half, axis=1), pltpu.roll(un, half, axis=1))
        o = un * cc + partner * ss
        if c < n_q_chunks:
            q_ref[:, c * LANES:(c + 1) * LANES] = (o * scale).astype(BF16)
        else:
            ck = c - n_q_chunks
            k_ref[:, ck * LANES:(ck + 1) * LANES] = o.astype(BF16)
    v_ref[...] = qkv[:, (n_q_chunks + n_k_chunks) * LANES:].astype(BF16)


def _qkv_rope(x2d, w_qkv, cc, ss, gq, gk, seg, seq_len):
    n, d = x2d.shape
    head_dim = d // N_HEADS
    dq = N_HEADS * head_dim
    dk = N_KV_HEADS * head_dim
    tm = ROW_TILE
    tiles_per_seq = seq_len // tm
    kern = functools.partial(_qkv_kernel, n_q_chunks=dq // LANES, n_k_chunks=dk // LANES,
                             head_dim=head_dim, scale=head_dim ** -0.5)
    return pl.pallas_call(
        kern,
        grid=(n // tm,),
        in_specs=[
            pl.BlockSpec((tm, d), lambda i: (i, 0)),
            pl.BlockSpec((d, dq + 2 * dk), lambda i: (0, 0)),
            pl.BlockSpec((tm, LANES), lambda i: (i % tiles_per_seq, 0)),
            pl.BlockSpec((tm, LANES), lambda i: (i % tiles_per_seq, 0)),
            pl.BlockSpec((1, LANES), lambda i: (0, 0)),
            pl.BlockSpec((1, LANES), lambda i: (0, 0)),
            pl.BlockSpec((LANES, LANES), lambda i: (0, 0)),
        ],
        out_specs=[
            pl.BlockSpec((tm, dq), lambda i: (i, 0)),
            pl.BlockSpec((tm, dk), lambda i: (i, 0)),
            pl.BlockSpec((tm, dk), lambda i: (i, 0)),
        ],
        out_shape=[
            jax.ShapeDtypeStruct((n, dq), BF16),
            jax.ShapeDtypeStruct((n, dk), BF16),
            jax.ShapeDtypeStruct((n, dk), BF16),
        ],
        compiler_params=_params(("parallel",)),
        name="attn_qkv_rope",
    )(x2d, w_qkv, cc, ss, gq, gk, seg)


def _attn_kernel(q_ref, k_ref, v_ref, o_ref, *, head_dim, group):
    tq = q_ref.shape[1]
    for kv in range(N_KV_HEADS):
        k = k_ref[0, :, kv * head_dim:(kv + 1) * head_dim]
        v = v_ref[0, :, kv * head_dim:(kv + 1) * head_dim]
        qs = [q_ref[0, :, (kv * group + g) * head_dim:(kv * group + g + 1) * head_dim] for g in range(group)]
        q = jnp.concatenate(qs, axis=0)
        s = lax.dot_general(q, k, (((1,), (1,)), ((), ())), preferred_element_type=F32)
        m = jnp.max(s, axis=-1, keepdims=True)
        p = jnp.exp(s - m)
        l = jnp.sum(p, axis=-1, keepdims=True)
        o = jnp.dot(p.astype(BF16), v, preferred_element_type=F32) / l
        outs = [o[g * tq:(g + 1) * tq, :] for g in range(group)]
        o_ref[0, :, kv * group * head_dim:(kv + 1) * group * head_dim] = (
            jnp.concatenate(outs, axis=1).astype(BF16))


def _attention(q, k, v, tq):
    b, l, dq = q.shape
    dk = k.shape[2]
    head_dim = dq // N_HEADS
    kern = functools.partial(_attn_kernel, head_dim=head_dim, group=N_HEADS // N_KV_HEADS)
    return pl.pallas_call(
        kern,
        grid=(b, l // tq),
        in_specs=[
            pl.BlockSpec((1, tq, dq), lambda i, j: (i, j, 0)),
            pl.BlockSpec((1, l, dk), lambda i, j: (i, 0, 0)),
            pl.BlockSpec((1, l, dk), lambda i, j: (i, 0, 0)),
        ],
        out_specs=pl.BlockSpec((1, tq, dq), lambda i, j: (i, j, 0)),
        out_shape=jax.ShapeDtypeStruct((b, l, dq), BF16),
        compiler_params=_params(("parallel", "parallel")),
        name="attn_core",
    )(q, k, v)


def _proj_ln_router_kernel(m_ref, w_ref, b_ref, x_ref, g_ref, beta_ref, rh_ref, rl_ref,
                           y_ref, lg_ref, *, alpha):
    mix = jnp.dot(m_ref[...], w_ref[...], preferred_element_type=F32) + b_ref[...]
    y = _layer_norm(alpha * x_ref[...] + mix, g_ref[...], beta_ref[...])
    y_ref[...] = y
    y_hi, y_lo = _split_bf16(y)
    rh = rh_ref[...]
    lg = (jnp.dot(y_hi, rh, preferred_element_type=F32)
          + jnp.dot(y_hi, rl_ref[...], preferred_element_type=F32)
          + jnp.dot(y_lo, rh, preferred_element_type=F32))
    ex = jnp.exp(lg - jnp.max(lg, axis=-1, keepdims=True))
    lg_ref[...] = ex / jnp.sum(ex, axis=-1, keepdims=True)


def _proj_ln_router(mix_in, w, bias, x2d, g, beta, r_hi, r_lo, alpha):
    n, d = x2d.shape
    dm = mix_in.shape[1]
    ne = r_hi.shape[1]
    tm = ROW_TILE
    kern = functools.partial(_proj_ln_router_kernel, alpha=alpha)
    return pl.pallas_call(
        kern,
        grid=(n // tm,),
        in_specs=[
            pl.BlockSpec((tm, dm), lambda i: (i, 0)),
            pl.BlockSpec((dm, d), lambda i: (0, 0)),
            pl.BlockSpec((1, d), lambda i: (0, 0)),
            pl.BlockSpec((tm, d), lambda i: (i, 0)),
            pl.BlockSpec((1, d), lambda i: (0, 0)),
            pl.BlockSpec((1, d), lambda i: (0, 0)),
            pl.BlockSpec((d, ne), lambda i: (0, 0)),
            pl.BlockSpec((d, ne), lambda i: (0, 0)),
        ],
        out_specs=[
            pl.BlockSpec((tm, d), lambda i: (i, 0)),
            pl.BlockSpec((tm, ne), lambda i: (i, 0)),
        ],
        out_shape=[
            jax.ShapeDtypeStruct((n, d), F32),
            jax.ShapeDtypeStruct((n, ne), F32),
        ],
        compiler_params=_params(("parallel",)),
        name="proj_ln_router",
    )(mix_in, w, bias, x2d, g, beta, r_hi, r_lo)


def _select_kernel(aff_ref, tri_ref, low_ref, slot_ref, gate_ref, base_ref, *, cap, slot_offset):
    aff = aff_ref[0]
    nt, tt = aff.shape
    bits = pltpu.bitcast(aff, I32)

    def count(mask):
        c = jnp.sum(mask.astype(F32), axis=1, keepdims=True)
        return jnp.sum(c, axis=0, keepdims=True)

    def step(i, prefix):
        cand = prefix | jnp.left_shift(jnp.int32(1), 30 - i)
        return jnp.where(count(bits >= cand) >= cap, cand, prefix)

    thr = lax.fori_loop(0, 31, step, jnp.zeros((1, 1), I32))
    gt = bits > thr
    eq = bits == thr
    need = cap - count(gt)

    tri = tri_ref[...]
    low = low_ref[...]
    ones = jnp.ones((tt, LANES), BF16)

    def excl_cumsum(mask):
        mf = mask.astype(F32).astype(BF16)
        within = jnp.dot(mf, tri, preferred_element_type=F32)
        tot = jnp.dot(mf, ones, preferred_element_type=F32).astype(BF16)
        base = jnp.dot(low, tot, preferred_element_type=F32)
        return within + base[:, :1], base

    rank_eq, _ = excl_cumsum(eq)
    sel = gt | (eq & (rank_eq < need))
    pos, base = excl_cumsum(sel)
    slot_ref[0] = jnp.where(sel, pos.astype(I32) + slot_offset, -1)
    gate_ref[0] = jnp.where(sel, aff, 0.0)
    base_ref[0] = base.astype(I32) + slot_offset


def _select(aff_t, cap, slot_offset):
    ne, nt, tt = aff_t.shape
    tri = (lax.broadcasted_iota(I32, (tt, tt), 0) < lax.broadcasted_iota(I32, (tt, tt), 1)).astype(BF16)
    low = (lax.broadcasted_iota(I32, (nt, nt), 0) > lax.broadcasted_iota(I32, (nt, nt), 1)).astype(BF16)
    kern = functools.partial(_select_kernel, cap=cap, slot_offset=slot_offset)
    return pl.pallas_call(
        kern,
        grid=(ne,),
        in_specs=[
            pl.BlockSpec((1, nt, tt), lambda e: (e, 0, 0)),
            pl.BlockSpec((tt, tt), lambda e: (0, 0)),
            pl.BlockSpec((nt, nt), lambda e: (0, 0)),
        ],
        out_specs=[
            pl.BlockSpec((1, nt, tt), lambda e: (e, 0, 0)),
            pl.BlockSpec((1, nt, tt), lambda e: (e, 0, 0)),
            pl.BlockSpec((1, nt, LANES), lambda e: (e, 0, 0)),
        ],
        out_shape=[
            jax.ShapeDtypeStruct((ne, nt, tt), I32),
            jax.ShapeDtypeStruct((ne, nt, tt), F32),
            jax.ShapeDtypeStruct((ne, nt, LANES), I32),
        ],
        compiler_params=_params(("parallel",)),
        name="moe_select",
    )(aff_t, tri, low)


def _onehot_rows(slot_row, first_slot):
    tt = slot_row.shape[1]
    rows = lax.broadcasted_iota(I32, (SLOT_BLOCK, tt), 0) + first_slot
    return rows == slot_row


def _dispatch_kernel(base_ref, x_ref, slot_ref, xe_ref, acc_ref, stage_ref, sem_ref, cnt_ref, *,
                     n_tiles, n_experts, ring):
    i = pl.program_id(0)

    @pl.when(i == 0)
    def _():
        acc_ref[...] = jnp.zeros_like(acc_ref)
        cnt_ref[0] = 0

    xb = x_ref[...].astype(BF16)
    slots = slot_ref[0]

    def block_copy(ring_slot, e, g):
        return pltpu.make_async_copy(stage_ref.at[ring_slot],
                                     xe_ref.at[e, pl.ds(g * SLOT_BLOCK, SLOT_BLOCK), :],
                                     sem_ref.at[ring_slot])

    def wait_slot(ring_slot):
        block_copy(ring_slot, cnt_ref[1 + ring_slot], cnt_ref[1 + ring + ring_slot]).wait()

    def emit(e, g, data):
        c = cnt_ref[0]
        ring_slot = c % ring

        @pl.when(c >= ring)
        def _():
            wait_slot(ring_slot)

        stage_ref[ring_slot] = data.astype(BF16)
        block_copy(ring_slot, e, g).start()
        cnt_ref[0] = c + 1
        cnt_ref[1 + ring_slot] = e
        cnt_ref[1 + ring + ring_slot] = g

    window = N_WIN * SLOT_BLOCK
    n_rounds = jnp.int32(1)
    for e in range(n_experts):
        b0 = base_ref[i * n_experts + e]
        b1 = base_ref[(i + 1) * n_experts + e]
        span = b1 - (b0 // SLOT_BLOCK) * SLOT_BLOCK
        n_rounds = jnp.maximum(n_rounds, (span + window - 1) // window)

    def do_round(r, carry):
        pieces = []
        for e in range(n_experts):
            g0 = base_ref[i * n_experts + e] // SLOT_BLOCK + r * N_WIN
            for j in range(N_WIN):
                pieces.append(_onehot_rows(slots[e:e + 1, :], (g0 + j) * SLOT_BLOCK))
        onehot = jnp.concatenate(pieces, axis=0).astype(F32).astype(BF16)
        contrib = jnp.dot(onehot, xb, preferred_element_type=F32)
        for e in range(n_experts):
            end = base_ref[(i + 1) * n_experts + e]
            g0 = base_ref[i * n_experts + e] // SLOT_BLOCK + r * N_WIN
            first = r == 0
            for j in range(N_WIN):
                row0 = (e * N_WIN + j) * SLOT_BLOCK
                data = contrib[row0:row0 + SLOT_BLOCK, :]
                g = g0 + j
                if j == 0:
                    data = data + jnp.where(first, acc_ref[e], 0.0)
                complete = end >= (g + 1) * SLOT_BLOCK
                partial = jnp.logical_and(end > g * SLOT_BLOCK, jnp.logical_not(complete))

                @pl.when(complete)
                def _(e=e, g=g, data=data):
                    emit(e, g, data)

                @pl.when(partial)
                def _(e=e, data=data):
                    acc_ref[e] = data

                if j == 0:
                    @pl.when(jnp.logical_and(first, complete))
                    def _(e=e):
                        acc_ref[e] = jnp.zeros(acc_ref.shape[1:], F32)
        return carry

    lax.fori_loop(0, n_rounds, do_round, 0)

    @pl.when(i == n_tiles - 1)
    def _():
        c = cnt_ref[0]
        for s in range(ring):
            @pl.when(c > s)
            def _(s=s):
                wait_slot(s)


def _dispatch(x2d, slots_t, base_flat, cap_total, ring=8):
    n, d = x2d.shape
    n_tiles, ne, tt = slots_t.shape
    kern = functools.partial(_dispatch_kernel, n_tiles=n_tiles, n_experts=ne, ring=ring)
    return pl.pallas_call(
        kern,
        grid_spec=pltpu.PrefetchScalarGridSpec(
            num_scalar_prefetch=1,
            grid=(n_tiles,),
            in_specs=[
                pl.BlockSpec((tt, d), lambda i, b: (i, 0)),
                pl.BlockSpec((1, ne, tt), lambda i, b: (i, 0, 0)),
            ],
            out_specs=pl.BlockSpec(memory_space=pl.ANY),
            scratch_shapes=[
                pltpu.VMEM((ne, SLOT_BLOCK, d), F32),
                pltpu.VMEM((ring, SLOT_BLOCK, d), BF16),
                pltpu.SemaphoreType.DMA((ring,)),
                pltpu.SMEM((1 + 2 * ring,), I32),
            ],
        ),
        out_shape=jax.ShapeDtypeStruct((ne, cap_total, d), BF16),
        compiler_params=_params(("arbitrary",)),
        name="moe_dispatch",
    )(base_flat, x2d, slots_t)


def _ffn_kernel(x_ref, wg_ref, wu_ref, wd_ref, o_ref, acc_ref):
    f = pl.program_id(2)

    @pl.when(f == 0)
    def _():
        acc_ref[...] = jnp.zeros_like(acc_ref)

    x = x_ref[0]
    hg = jnp.dot(x, wg_ref[0], preferred_element_type=F32)
    hu = jnp.dot(x, wu_ref[0], preferred_element_type=F32)
    h = (hg * jax.nn.sigmoid(hg) * hu).astype(BF16)
    acc_ref[...] += jnp.dot(h, wd_ref[0], preferred_element_type=F32)

    @pl.when(f == pl.num_programs(2) - 1)
    def _():
        o_ref[0] = acc_ref[...].astype(BF16)


def _ffn_tiles(cap_total, ff):
    tm = 1024
    while cap_total % tm:
        tm //= 2
    tf = 256 if ff % 256 == 0 else LANES
    return tm, tf


def _expert_ffn(xe, wg, wu, wd):
    ne, cap_total, d = xe.shape
    ff = wg.shape[2]
    tm, tf = _ffn_tiles(cap_total, ff)
    return pl.pallas_call(
        _ffn_kernel,
        grid=(ne, cap_total // tm, ff // tf),
        in_specs=[
            pl.BlockSpec((1, tm, d), lambda e, i, f: (e, i, 0)),
            pl.BlockSpec((1, d, tf), lambda e, i, f: (e, 0, f)),
            pl.BlockSpec((1, d, tf), lambda e, i, f: (e, 0, f)),
            pl.BlockSpec((1, tf, d), lambda e, i, f: (e, f, 0)),
        ],
        out_specs=pl.BlockSpec((1, tm, d), lambda e, i, f: (e, i, 0)),
        out_shape=jax.ShapeDtypeStruct((ne, cap_total, d), BF16),
        scratch_shapes=[pltpu.VMEM((tm, d), F32)],
        compiler_params=_params(("parallel", "parallel", "arbitrary")),
        name="moe_expert_ffn",
    )(xe, wg, wu, wd)


def _combine_kernel(base_ref, *refs, n_experts, n_blocks_total, alpha):
    nwin = n_experts * N_WIN
    blk_refs = refs[:nwin]
    slot_ref, gate_ref, x_ref, g_ref, beta_ref, ye_ref, o_ref, extra_ref, sem_ref = refs[nwin:]
    i = pl.program_id(0)
    slots = slot_ref[0]
    gates = gate_ref[0]
    window = N_WIN * SLOT_BLOCK

    def weights(r):
        pieces = []
        for e in range(n_experts):
            g0 = base_ref[i * n_experts + e] // SLOT_BLOCK + r * N_WIN
            for j in range(N_WIN):
                hit = _onehot_rows(slots[e:e + 1, :], (g0 + j) * SLOT_BLOCK)
                pieces.append(jnp.where(hit, gates[e:e + 1, :], 0.0))
        return jnp.concatenate(pieces, axis=0).T.astype(BF16)

    ye0 = jnp.concatenate([r[0] for r in blk_refs], axis=0)
    y = jnp.dot(weights(0), ye0, preferred_element_type=F32)

    n_rounds = jnp.int32(1)
    for e in range(n_experts):
        b0 = base_ref[i * n_experts + e]
        b1 = base_ref[(i + 1) * n_experts + e]
        span = b1 - (b0 // SLOT_BLOCK) * SLOT_BLOCK
        n_rounds = jnp.maximum(n_rounds, (span + window - 1) // window)

    def extra_round(r, y):
        for e in range(n_experts):
            g0 = base_ref[i * n_experts + e] // SLOT_BLOCK + r * N_WIN
            for j in range(N_WIN):
                g = jnp.minimum(g0 + j, n_blocks_total - 1)
                cp = pltpu.make_async_copy(ye_ref.at[e, pl.ds(g * SLOT_BLOCK, SLOT_BLOCK), :],
                                           extra_ref.at[pl.ds((e * N_WIN + j) * SLOT_BLOCK, SLOT_BLOCK), :],
                                           sem_ref.at[0])
                cp.start()
                cp.wait()
        return y + jnp.dot(weights(r), extra_ref[...], preferred_element_type=F32)

    y = lax.fori_loop(1, n_rounds, extra_round, y)
    o_ref[...] = _layer_norm(alpha * x_ref[...] + y, g_ref[...], beta_ref[...])


def _combine_ln(ye, slots_t, gates_t, base_flat, x2d, g, beta, alpha):
    n, d = x2d.shape
    n_tiles, ne, tt = slots_t.shape
    cap_total = ye.shape[1]
    n_blocks_total = cap_total // SLOT_BLOCK
    kern = functools.partial(_combine_kernel, n_experts=ne, n_blocks_total=n_blocks_total, alpha=alpha)

    def blk_spec(e, j):
        def imap(i, b):
            return (e, jnp.minimum(b[i * ne + e] // SLOT_BLOCK + j, n_blocks_total - 1), 0)
        return pl.BlockSpec((1, SLOT_BLOCK, d), imap)

    in_specs = [blk_spec(e, j) for e in range(ne) for j in range(N_WIN)]
    in_specs += [
        pl.BlockSpec((1, ne, tt), lambda i, b: (i, 0, 0)),
        pl.BlockSpec((1, ne, tt), lambda i, b: (i, 0, 0)),
        pl.BlockSpec((tt, d), lambda i, b: (i, 0)),
        pl.BlockSpec((1, d), lambda i, b: (0, 0)),
        pl.BlockSpec((1, d), lambda i, b: (0, 0)),
        pl.BlockSpec(memory_space=pl.ANY),
    ]
    return pl.pallas_call(
        kern,
        grid_spec=pltpu.PrefetchScalarGridSpec(
            num_scalar_prefetch=1,
            grid=(n_tiles,),
            in_specs=in_specs,
            out_specs=pl.BlockSpec((tt, d), lambda i, b: (i, 0)),
            scratch_shapes=[
                pltpu.VMEM((ne * N_WIN * SLOT_BLOCK, d), BF16),
                pltpu.SemaphoreType.DMA((1,)),
            ],
        ),
        out_shape=jax.ShapeDtypeStruct((n, d), F32),
        compiler_params=_params(("arbitrary",)),
        name="moe_combine_ln",
    )(base_flat, *([ye] * (ne * N_WIN)), slots_t, gates_t, x2d, g, beta, ye)


def _filter_kernel(w1t_ref, w1c_ref, w1s_ref, b1_ref, fq_ref, w2_ref, b2_ref, w3_ref, fr_ref, dl_ref,
                   o_ref, *, seq_len):
    half = pl.program_id(0)
    pos = lax.broadcasted_iota(I32, (1, seq_len), 1)
    lag = jnp.where(half == 0, seq_len - pos, pos)
    lagf = lag.astype(F32)
    t = lagf / (seq_len - 1.0)
    w = (2.0 * math.pi) * lagf / seq_len
    fw = fr_ref[...] * w
    hp = lax.Precision.HIGHEST
    pre = (w1t_ref[...] * t
           + jnp.dot(w1c_ref[...], jnp.cos(fw), precision=hp, preferred_element_type=F32)
           - jnp.dot(w1s_ref[...], jnp.sin(fw), precision=hp, preferred_element_type=F32)
           + b1_ref[...])
    fq = fq_ref[...]
    h = jnp.sin(fq * pre)
    h = jnp.sin(fq * (jnp.dot(w2_ref[...], h, precision=hp, preferred_element_type=F32) + b2_ref[...]))
    out = jnp.dot(w3_ref[0], h, precision=hp, preferred_element_type=F32)
    out = out * jnp.exp(-t * dl_ref[...])
    o_ref[...] = jnp.where(lag < seq_len, out, 0.0)


def _hyena_filter(f_w1, f_b1, f_freq, f_w2, f_b2, f_w3, seq_len, d):
    hid = f_w1.shape[1]
    bands = (FILTER_EMB - 1) // 2
    w1 = f_w1.astype(F32).T
    col = lambda v: v.astype(F32).reshape(-1, 1)
    freqs = jnp.linspace(1e-4, bands - 1, bands, dtype=F32).reshape(bands, 1)
    min_decay = math.log(DECAY_TARGET) / SLOW_DECAY_PCT
    max_decay = math.log(DECAY_TARGET) / FAST_DECAY_PCT
    deltas = jnp.abs(jnp.linspace(min_decay, max_decay, d, dtype=F32)).reshape(d, 1)
    w3 = f_w3.astype(F32).T.reshape(2, d, hid)
    full = lambda shape: pl.BlockSpec(shape, lambda s: (0,) * len(shape))
    return pl.pallas_call(
        functools.partial(_filter_kernel, seq_len=seq_len),
        grid=(2,),
        in_specs=[
            full((hid, 1)), full((hid, bands)), full((hid, bands)), full((hid, 1)), full((hid, 1)),
            full((hid, hid)), full((hid, 1)),
            pl.BlockSpec((1, d, hid), lambda s: (1 - s, 0, 0)),
            full((bands, 1)), full((d, 1)),
        ],
        out_specs=pl.BlockSpec((d, seq_len), lambda s: (0, s)),
        out_shape=jax.ShapeDtypeStruct((d, 2 * seq_len), F32),
        compiler_params=_params(("arbitrary",)),
        name="hyena_filter",
    )(w1[:, :1], w1[:, 1:1 + bands], w1[:, 1 + bands:], col(f_b1), col(f_freq),
      f_w2.astype(F32).T, col(f_b2), w3, freqs, deltas)


def _hyena_in_kernel(x_ref, w0_ref, w1_ref, w2_ref, b_ref, cw_ref, cb_ref, x0_ref, z_ref, xb_ref):
    c = pl.program_id(1)

    @pl.when(c == 0)
    def _():
        xb_ref[...] = x_ref[0].astype(BF16)

    xb = xb_ref[...]
    seq_len = xb.shape[0]
    row = lax.broadcasted_iota(I32, (seq_len, 1), 0)
    not_first = row > 0
    not_last = row < seq_len - 1

    def branch(w_ref, k):
        u = jnp.dot(xb, w_ref[...], preferred_element_type=F32) + b_ref[k]
        prev = jnp.where(not_first, pltpu.roll(u, 1, axis=0), 0.0)
        nxt = jnp.where(not_last, pltpu.roll(u, seq_len - 1, axis=0), 0.0)
        cw = cw_ref[k]
        return prev * cw[0:1] + u * cw[1:2] + nxt * cw[2:3] + cb_ref[k]

    x0_ref[0] = branch(w0_ref, 0)
    z_ref[0] = branch(w2_ref, 2) * branch(w1_ref, 1)


def _hyena_in(x3d, w_in, b_in, conv_w, conv_b):
    b, l, d = x3d.shape
    ch = HY_CH
    nc = d // ch
    b3 = b_in.astype(F32).reshape(3, 1, d)
    cw3 = conv_w.astype(F32).reshape(conv_w.shape[0], 3, d).transpose(1, 0, 2)
    cb3 = conv_b.astype(F32).reshape(3, 1, d)
    wspec = lambda k: pl.BlockSpec((d, ch), lambda i, c, k=k: (0, k * nc + c))
    return pl.pallas_call(
        _hyena_in_kernel,
        grid=(b, nc),
        in_specs=[
            pl.BlockSpec((1, l, d), lambda i, c: (i, 0, 0)),
            wspec(0), wspec(1), wspec(2),
            pl.BlockSpec((3, 1, ch), lambda i, c: (0, 0, c)),
            pl.BlockSpec((3, 3, ch), lambda i, c: (0, 0, c)),
            pl.BlockSpec((3, 1, ch), lambda i, c: (0, 0, c)),
        ],
        out_specs=[
            pl.BlockSpec((1, l, ch), lambda i, c: (i, 0, c)),
            pl.BlockSpec((1, l, ch), lambda i, c: (i, 0, c)),
        ],
        out_shape=[
            jax.ShapeDtypeStruct((b, l, d), F32),
            jax.ShapeDtypeStruct((b, l, d), F32),
        ],
        scratch_shapes=[pltpu.VMEM((l, d), BF16)],
        compiler_params=_params(("parallel", "arbitrary")),
        name="hyena_in_conv",
    )(x3d, w_in, w_in, w_in, b3, cw3, cb3)


def _long_conv_kernel(z_ref, x0_ref, kk_ref, skip_ref, o_ref, *, n_blk):
    blk = CONV_BLOCK
    nb = z_ref.shape[1]

    def channel(c, carry):
        zc = z_ref[c]
        zr = jnp.concatenate([zc[:, j * blk:(j + 1) * blk] for j in range(n_blk)], axis=0).astype(BF16)
        kk = kk_ref[pl.ds(c, 1), :]
        acc = [None] * n_blk
        for delta in range(-(n_blk - 1), n_blk):
            start = (n_blk + delta - 1) * blk
            wrow = jnp.broadcast_to(kk[:, start:start + 2 * blk], (blk, 2 * blk))
            rolled = pltpu.roll(wrow, 0, axis=1, stride=1, stride_axis=0)
            tt = rolled[:, blk:].astype(BF16)
            j0, j1 = max(0, -delta), min(n_blk, n_blk - delta)
            res = jnp.dot(zr[j0 * nb:j1 * nb, :], tt, preferred_element_type=F32)
            for j in range(j0, j1):
                part = res[(j - j0) * nb:(j - j0 + 1) * nb, :]
                i = j + delta
                acc[i] = part if acc[i] is None else acc[i] + part
        y = jnp.concatenate(acc, axis=1)
        o_ref[c] = (y + zc * skip_ref[pl.ds(c, 1), :]) * x0_ref[c]
        return carry

    lax.fori_loop(0, z_ref.shape[0], channel, 0)


def _long_conv(z_t, x0_t, kk, skip):
    d, b, l = z_t.shape
    ch = CONV_CH
    kern = functools.partial(_long_conv_kernel, n_blk=l // CONV_BLOCK)
    return pl.pallas_call(
        kern,
        grid=(d // ch,),
        in_specs=[
            pl.BlockSpec((ch, b, l), lambda i: (i, 0, 0)),
            pl.BlockSpec((ch, b, l), lambda i: (i, 0, 0)),
            pl.BlockSpec((ch, 2 * l), lambda i: (i, 0)),
            pl.BlockSpec((ch, 1), lambda i: (i, 0)),
        ],
        out_specs=pl.BlockSpec((ch, b, l), lambda i: (i, 0, 0)),
        out_shape=jax.ShapeDtypeStruct((d, b, l), F32),
        compiler_params=_params(("parallel",)),
        name="hyena_long_conv",
    )(z_t, x0_t, kk, skip)


def _rope_tables(seq_len, head_dim):
    rows = seq_len // GRID_W
    row = jnp.repeat(jnp.arange(rows), GRID_W)
    col = jnp.tile(jnp.arange(GRID_W), rows)
    axis_dim = head_dim // 2
    inv = ROPE_THETA ** (-jnp.arange(0, axis_dim, 2, dtype=F32) / axis_dim)
    ang = jnp.concatenate([row[:, None] * inv, col[:, None] * inv], -1)
    cos, sin = jnp.cos(ang), jnp.sin(ang)
    reps = LANES // head_dim
    cc = jnp.tile(jnp.concatenate([cos, cos], -1), (1, reps))
    ss = jnp.tile(jnp.concatenate([-sin, sin], -1), (1, reps))
    return cc, ss


def _moe(x1, logits, group_tokens, w_gate, w_up, w_down, g, beta, alpha):
    n, d = x1.shape
    ne = logits.shape[1]
    tt = TOK_TILE
    slots, gates, bases = [], [], []
    tok0, slot0 = 0, 0
    for ng in group_tokens:
        cap = EC_CAPACITY * ng // ne
        lt = logits[tok0:tok0 + ng].T.reshape(ne, ng // tt, tt)
        s, gt, bs = _select(lt, cap, slot0)
        slots.append(s)
        gates.append(gt)
        bases.append(bs[:, :, 0])
        tok0 += ng
        slot0 += cap
    cap_total = slot0
    slots_t = jnp.concatenate(slots, axis=1).transpose(1, 0, 2)
    gates_t = jnp.concatenate(gates, axis=1).transpose(1, 0, 2)
    base = jnp.concatenate(bases + [jnp.full((ne, 1), cap_total, I32)], axis=1).T
    base_flat = base.reshape(-1)
    xe = _dispatch(x1, slots_t, base_flat, cap_total)
    ye = _expert_ffn(xe, w_gate, w_up, w_down)
    return _combine_ln(ye, slots_t, gates_t, base_flat, x1, g, beta, alpha)


def kernel(x_prompt, x_sample, attn_w_qkv, attn_q_gain, attn_k_gain, attn_w_o, hy_w_in, hy_b_in, hy_conv_w, hy_conv_b, hy_f_w1, hy_f_b1, hy_f_freq, hy_f_w2, hy_f_b2, hy_f_w3, hy_skip, hy_w_out, hy_b_out, ln_mix_g, ln_mix_b, moe_router, moe_w_gate, moe_w_up, moe_w_down, ln_ffn_g, ln_ffn_b):
    bp, l, d = x_prompt.shape
    bs = x_sample.shape[0]
    assert x_sample.shape[1] == l
    depth = ln_mix_g.shape[0]
    alpha = (2 * depth) ** 0.25
    head_dim = d // N_HEADS
    b = bp + bs
    n = b * l
    group_tokens = (bp * l, bs * l)

    x = jnp.concatenate([x_prompt, x_sample], axis=0).reshape(n, d)
    cc, ss = _rope_tables(l, head_dim)
    seg = (lax.broadcasted_iota(I32, (LANES, LANES), 0) // head_dim
           == lax.broadcasted_iota(I32, (LANES, LANES), 1) // head_dim).astype(BF16)
    row = lambda v: v.astype(F32).reshape(1, -1)
    zero_bias = jnp.zeros((1, d), F32)

    for i in range(depth):
        j = i // 2
        r_hi, r_lo = _split_bf16(moe_router[i].astype(F32))
        if i % 2 == 0:
            gq = jnp.tile(row(attn_q_gain[j]), (1, LANES // head_dim))
            gk = jnp.tile(row(attn_k_gain[j]), (1, LANES // head_dim))
            q, k, v = _qkv_rope(x, attn_w_qkv[j].astype(BF16), cc, ss, gq, gk, seg, l)
            o = _attention(q.reshape(b, l, -1), k.reshape(b, l, -1), v.reshape(b, l, -1), tq=min(256, l))
            mix_in, w_mix, b_mix = o.reshape(n, -1), attn_w_o[j].astype(BF16), zero_bias
        else:
            kk = _hyena_filter(hy_f_w1[j], hy_f_b1[j], hy_f_freq[j], hy_f_w2[j], hy_f_b2[j], hy_f_w3[j], l, d)
            x0, z = _hyena_in(x.reshape(b, l, d), hy_w_in[j].astype(BF16), hy_b_in[j], hy_conv_w[j], hy_conv_b[j])
            y_t = _long_conv(z.transpose(2, 0, 1), x0.transpose(2, 0, 1), kk,
                             hy_skip[j].astype(F32).reshape(d, 1))
            mix_in = y_t.transpose(1, 2, 0).astype(BF16).reshape(n, d)
            w_mix, b_mix = hy_w_out[j].astype(BF16), row(hy_b_out[j])
        x1, logits = _proj_ln_router(mix_in, w_mix, b_mix, x, row(ln_mix_g[i]), row(ln_mix_b[i]),
                                     r_hi, r_lo, alpha)
        x = _moe(x1, logits, group_tokens, moe_w_gate[i].astype(BF16), moe_w_up[i].astype(BF16),
                 moe_w_down[i].astype(BF16), row(ln_ffn_g[i]), row(ln_ffn_b[i]), alpha)

    y = x.reshape(b, l, d)
    return (y[:bp], y[bp:])
```

```python
import functools
import math

import jax
import jax.numpy as jnp
from jax import lax
from jax.experimental import pallas as pl
from jax.experimental.pallas import tpu as pltpu

F32 = jnp.float32
BF16 = jnp.bfloat16
I32 = jnp.int32

N_HEADS = 16
N_KV_HEADS = 4
GRID_W = 64
ROPE_THETA = 10000.0
QK_EPS = 1e-6
FILTER_EMB = 33
DECAY_TARGET = 1e-2
FAST_DECAY_PCT = 0.3
SLOW_DECAY_PCT = 1.5
EC_CAPACITY = 2
LN_EPS = 1e-5

LANES = 128
MXU_DIM = 256
VMEM_LIMIT_BYTES = 56 * 1024 * 1024

ROW_TILE = 256
TOK_TILE = 256
SLOT_ALIGN = 16
SLOT_WIN = 64
FFN_ROWS = 512
CONV_BLOCK = 256
CONV_CH = 8
HY_CH = 128


def _params(sem, vmem=VMEM_LIMIT_BYTES):
    return pltpu.CompilerParams(dimension_semantics=sem, vmem_limit_bytes=vmem)


def _layer_norm(v, g, b):
    mu = jnp.mean(v, axis=-1, keepdims=True)
    c = v - mu
    var = jnp.mean(c * c, axis=-1, keepdims=True)
    return c * lax.rsqrt(var + LN_EPS) * g + b


def _split_bf16(v):
    hi = v.astype(BF16)
    lo = (v - hi.astype(F32)).astype(BF16)
    return hi, lo


def _qkv_kernel(x_ref, w_ref, cc_ref, ss_ref, gq_ref, gk_ref, seg_ref, q_ref, k_ref, v_ref, *,
                n_q_chunks, n_k_chunks, head_dim, scale):
    xb = x_ref[...].astype(BF16)
    qkv = jnp.dot(xb, w_ref[...], preferred_element_type=F32)
    cc = cc_ref[...]
    ss = ss_ref[...]
    seg = seg_ref[...]
    lane = lax.broadcasted_iota(I32, (xb.shape[0], LANES), 1)
    first_half = (lane % head_dim) < (head_dim // 2)
    half = head_dim // 2
    for c in range(n_q_chunks + n_k_chunks):
        u = qkv[:, c * LANES:(c + 1) * LANES]
        s_hi, s_lo = _split_bf16(u * u)
        ssum = (jnp.dot(s_hi, seg, preferred_element_type=F32)
                + jnp.dot(s_lo, seg, preferred_element_type=F32))
        r = lax.rsqrt(ssum * (1.0 / head_dim) + QK_EPS)
        gain = gq_ref[...] if c < n_q_chunks else gk_ref[...]
        un = u * r * gain
        partner = jnp.where(first_half, pltpu.roll(un, LANES - half, axis=1), pltpu.roll(un, half, axis=1))
        o = un * cc + partner * ss
        if c < n_q_chunks:
            q_ref[:, c * LANES:(c + 1) * LANES] = (o * scale).astype(BF16)
        else:
            ck = c - n_q_chunks
            k_ref[:, ck * LANES:(ck + 1) * LANES] = o.astype(BF16)
    v_ref[...] = qkv[:, (n_q_chunks + n_k_chunks) * LANES:].astype(BF16)


def _qkv_rope(x2d, w_qkv, cc, ss, gq, gk, seg, seq_len):
    n, d = x2d.shape
    head_dim = d // N_HEADS
    dq = N_HEADS * head_dim
    dk = N_KV_HEADS * head_dim
    tm = ROW_TILE
    tiles_per_seq = seq_len // tm
    kern = functools.partial(_qkv_kernel, n_q_chunks=dq // LANES, n_k_chunks=dk // LANES,
                             head_dim=head_dim, scale=head_dim ** -0.5)
    return pl.pallas_call(
        kern,
        grid=(n // tm,),
        in_specs=[
            pl.BlockSpec((tm, d), lambda i: (i, 0)),
            pl.BlockSpec((d, dq + 2 * dk), lambda i: (0, 0)),
            pl.BlockSpec((tm, LANES), lambda i: (i % tiles_per_seq, 0)),
            pl.BlockSpec((tm, LANES), lambda i: (i % tiles_per_seq, 0)),
            pl.BlockSpec((1, LANES), lambda i: (0, 0)),
            pl.BlockSpec((1, LANES), lambda i: (0, 0)),
            pl.BlockSpec((LANES, LANES), lambda i: (0, 0)),
        ],
        out_specs=[
            pl.BlockSpec((tm, dq), lambda i: (i, 0)),
            pl.BlockSpec((tm, dk), lambda i: (i, 0)),
            pl.BlockSpec((tm, dk), lambda i: (i, 0)),
        ],
        out_shape=[
            jax.ShapeDtypeStruct((n, dq), BF16),
            jax.ShapeDtypeStruct((n, dk), BF16),
            jax.ShapeDtypeStruct((n, dk), BF16),
        ],
        compiler_params=_params(("parallel",)),
        name="attn_qkv_rope",
    )(x2d, w_qkv, cc, ss, gq, gk, seg)


def _attn_kernel(q_ref, k_ref, v_ref, o_ref, *, head_dim, group):
    tq = q_ref.shape[1]
    for kv in range(N_KV_HEADS):
        k = k_ref[0, :, kv * head_dim:(kv + 1) * head_dim]
        v = v_ref[0, :, kv * head_dim:(kv + 1) * head_dim]
        qs = [q_ref[0, :, (kv * group + g) * head_dim:(kv * group + g + 1) * head_dim] for g in range(group)]
        q = jnp.concatenate(qs, axis=0)
        s = lax.dot_general(q, k, (((1,), (1,)), ((), ())), preferred_element_type=F32)
        m = jnp.max(s, axis=-1, keepdims=True)
        p = jnp.exp(s - m)
        l = jnp.sum(p, axis=-1, keepdims=True)
        o = jnp.dot(p.astype(BF16), v, preferred_element_type=F32) / l
        outs = [o[g * tq:(g + 1) * tq, :] for g in range(group)]
        o_ref[0, :, kv * group * head_dim:(kv + 1) * group * head_dim] = (
            jnp.concatenate(outs, axis=1).astype(BF16))


def _attention(q, k, v, tq):
    b, l, dq = q.shape
    dk = k.shape[2]
    head_dim = dq // N_HEADS
    kern = functools.partial(_attn_kernel, head_dim=head_dim, group=N_HEADS // N_KV_HEADS)
    return pl.pallas_call(
        kern,
        grid=(b, l // tq),
        in_specs=[
            pl.BlockSpec((1, tq, dq), lambda i, j: (i, j, 0)),
            pl.BlockSpec((1, l, dk), lambda i, j: (i, 0, 0)),
            pl.BlockSpec((1, l, dk), lambda i, j: (i, 0, 0)),
        ],
        out_specs=pl.BlockSpec((1, tq, dq), lambda i, j: (i, j, 0)),
        out_shape=jax.ShapeDtypeStruct((b, l, dq), BF16),
        compiler_params=_params(("parallel", "parallel")),
        name="attn_core",
    )(q, k, v)


def _proj_ln_router_kernel(m_ref, w_ref, b_ref, x_ref, g_ref, beta_ref, rh_ref, rl_ref,
                           y_ref, aff_ref, *, alpha):
    mix = jnp.dot(m_ref[...], w_ref[...], preferred_element_type=F32) + b_ref[...]
    y = _layer_norm(alpha * x_ref[...] + mix, g_ref[...], beta_ref[...])
    y_ref[...] = y
    y_hi, y_lo = _split_bf16(y)
    rh = rh_ref[...]
    lg = (jnp.dot(y_hi, rh, preferred_element_type=F32)
          + jnp.dot(y_hi, rl_ref[...], preferred_element_type=F32)
          + jnp.dot(y_lo, rh, preferred_element_type=F32))
    ex = jnp.exp(lg - jnp.max(lg, axis=-1, keepdims=True))
    aff_ref[...] = ex / jnp.sum(ex, axis=-1, keepdims=True)


def _proj_ln_router(mix_in, w, bias, x2d, g, beta, r_hi, r_lo, alpha):
    n, d = x2d.shape
    dm = mix_in.shape[1]
    ne = r_hi.shape[1]
    tm = ROW_TILE
    kern = functools.partial(_proj_ln_router_kernel, alpha=alpha)
    return pl.pallas_call(
        kern,
        grid=(n // tm,),
        in_specs=[
            pl.BlockSpec((tm, dm), lambda i: (i, 0)),
            pl.BlockSpec((dm, d), lambda i: (0, 0)),
            pl.BlockSpec((1, d), lambda i: (0, 0)),
            pl.BlockSpec((tm, d), lambda i: (i, 0)),
            pl.BlockSpec((1, d), lambda i: (0, 0)),
            pl.BlockSpec((1, d), lambda i: (0, 0)),
            pl.BlockSpec((d, ne), lambda i: (0, 0)),
            pl.BlockSpec((d, ne), lambda i: (0, 0)),
        ],
        out_specs=[
            pl.BlockSpec((tm, d), lambda i: (i, 0)),
            pl.BlockSpec((tm, ne), lambda i: (i, 0)),
        ],
        out_shape=[
            jax.ShapeDtypeStruct((n, d), F32),
            jax.ShapeDtypeStruct((n, ne), F32),
        ],
        compiler_params=_params(("parallel",)),
        name="proj_ln_router",
    )(mix_in, w, bias, x2d, g, beta, r_hi, r_lo)


def _select_kernel(aff_ref, tri_ref, low_ref, slot_ref, gate_ref, base_ref, *, cap, slot_offset):
    aff = aff_ref[0]
    nt, tt = aff.shape
    bits = pltpu.bitcast(aff, I32)

    def count(mask):
        c = jnp.sum(mask.astype(F32), axis=1, keepdims=True)
        return jnp.sum(c, axis=0, keepdims=True)

    def step(i, prefix):
        cand = prefix | jnp.left_shift(jnp.int32(1), 30 - i)
        return jnp.where(count(bits >= cand) >= cap, cand, prefix)

    thr = lax.fori_loop(0, 31, step, jnp.zeros((1, 1), I32))
    gt = bits > thr
    eq = bits == thr
    need = cap - count(gt)

    tri = tri_ref[...]
    low = low_ref[...]
    ones = jnp.ones((tt, LANES), BF16)

    def excl_cumsum(mask):
        mf = mask.astype(F32).astype(BF16)
        within = jnp.dot(mf, tri, preferred_element_type=F32)
        tot = jnp.dot(mf, ones, preferred_element_type=F32).astype(BF16)
        base = jnp.dot(low, tot, preferred_element_type=F32)
        return within + base[:, :1], base

    rank_eq, _ = excl_cumsum(eq)
    sel = gt | (eq & (rank_eq < need))
    pos, base = excl_cumsum(sel)
    slot_ref[0] = jnp.where(sel, pos.astype(I32) + slot_offset, -1)
    gate_ref[0] = jnp.where(sel, aff, 0.0)
    base_ref[0] = base.astype(I32) + slot_offset


def _select(aff_t, cap, slot_offset):
    ne, nt, tt = aff_t.shape
    tri = (lax.broadcasted_iota(I32, (tt, tt), 0) < lax.broadcasted_iota(I32, (tt, tt), 1)).astype(BF16)
    low = (lax.broadcasted_iota(I32, (nt, nt), 0) > lax.broadcasted_iota(I32, (nt, nt), 1)).astype(BF16)
    kern = functools.partial(_select_kernel, cap=cap, slot_offset=slot_offset)
    return pl.pallas_call(
        kern,
        grid=(ne,),
        in_specs=[
            pl.BlockSpec((1, nt, tt), lambda e: (e, 0, 0)),
            pl.BlockSpec((tt, tt), lambda e: (0, 0)),
            pl.BlockSpec((nt, nt), lambda e: (0, 0)),
        ],
        out_specs=[
            pl.BlockSpec((1, nt, tt), lambda e: (e, 0, 0)),
            pl.BlockSpec((1, nt, tt), lambda e: (e, 0, 0)),
            pl.BlockSpec((1, nt, LANES), lambda e: (e, 0, 0)),
        ],
        out_shape=[
            jax.ShapeDtypeStruct((ne, nt, tt), I32),
            jax.ShapeDtypeStruct((ne, nt, tt), F32),
            jax.ShapeDtypeStruct((ne, nt, LANES), I32),
        ],
        compiler_params=_params(("parallel",)),
        name="moe_select",
    )(aff_t, tri, low)


def _onehot_rows(slot_row, first_slot, n_rows):
    tt = slot_row.shape[1]
    rows = lax.broadcasted_iota(I32, (n_rows, tt), 0) + first_slot
    return rows == slot_row


def _window_rounds(base_ref, i, n_experts, strict):
    n_rounds = jnp.int32(1)
    for e in range(n_experts):
        b0 = base_ref[i * n_experts + e]
        b1 = base_ref[(i + 1) * n_experts + e]
        span = b1 - (b0 // SLOT_ALIGN) * SLOT_ALIGN
        r = span // SLOT_WIN + 1 if strict else (span + SLOT_WIN - 1) // SLOT_WIN
        n_rounds = jnp.maximum(n_rounds, r)
    return n_rounds


def _dispatch_kernel(base_ref, x_ref, slot_ref, xe_ref, carry_ref, stage_ref, sem_ref, last_ref, *,
                     n_tiles, n_experts, cap_total, pad_windows):
    i = pl.program_id(0)

    def window_copy(e, start):
        rows = pl.ds(pl.multiple_of(start, SLOT_ALIGN), SLOT_WIN)
        return pltpu.make_async_copy(stage_ref.at[e], xe_ref.at[e, rows, :], sem_ref.at[e])

    @pl.when(i == 0)
    def _():
        carry_ref[...] = jnp.zeros_like(carry_ref)
        stage_ref[...] = jnp.zeros_like(stage_ref)
        pads = [window_copy(e, cap_total + k * SLOT_WIN) for e in range(n_experts) for k in range(pad_windows)]
        for cp in pads:
            cp.start()
        for cp in pads:
            cp.wait()

    xb = x_ref[...].astype(BF16)
    slots = slot_ref[0]

    def do_round(r, carry):
        starts = [(base_ref[i * n_experts + e] // SLOT_ALIGN) * SLOT_ALIGN + r * SLOT_WIN
                  for e in range(n_experts)]
        onehot = jnp.concatenate(
            [_onehot_rows(slots[e:e + 1, :], starts[e], SLOT_WIN) for e in range(n_experts)], axis=0)
        contrib = jnp.dot(onehot.astype(F32).astype(BF16), xb, preferred_element_type=F32)
        for e in range(n_experts):
            data = contrib[e * SLOT_WIN:(e + 1) * SLOT_WIN, :].astype(BF16)
            head = jnp.where(r == 0, carry_ref[e], jnp.zeros_like(carry_ref[e]))

            @pl.when(jnp.logical_or(i > 0, r > 0))
            def _(e=e):
                window_copy(e, last_ref[e]).wait()

            stage_ref[e, :SLOT_ALIGN, :] = data[:SLOT_ALIGN, :] + head
            stage_ref[e, SLOT_ALIGN:, :] = data[SLOT_ALIGN:, :]
            window_copy(e, starts[e]).start()
            last_ref[e] = starts[e]
            rem = base_ref[(i + 1) * n_experts + e] - starts[e]
            k = jnp.clip(rem // SLOT_ALIGN, 0, SLOT_WIN // SLOT_ALIGN - 1)
            tail = stage_ref[e, pl.ds(pl.multiple_of(k * SLOT_ALIGN, SLOT_ALIGN), SLOT_ALIGN), :]
            is_last = jnp.logical_and(rem >= 0, rem < SLOT_WIN)
            carry_ref[e] = jnp.where(is_last, tail, carry_ref[e])
        return carry

    lax.fori_loop(0, _window_rounds(base_ref, i, n_experts, strict=True), do_round, 0)

    @pl.when(i == n_tiles - 1)
    def _():
        for e in range(n_experts):
            window_copy(e, last_ref[e]).wait()


def _dispatch(x2d, slots_t, base_flat, cap_total):
    n, d = x2d.shape
    n_tiles, ne, tt = slots_t.shape
    pad_windows = tt // SLOT_WIN + 1
    pad_rows = pad_windows * SLOT_WIN
    kern = functools.partial(_dispatch_kernel, n_tiles=n_tiles, n_experts=ne, cap_total=cap_total,
                             pad_windows=pad_windows)
    return pl.pallas_call(
        kern,
        grid_spec=pltpu.PrefetchScalarGridSpec(
            num_scalar_prefetch=1,
            grid=(n_tiles,),
            in_specs=[
                pl.BlockSpec((tt, d), lambda i, b: (i, 0)),
                pl.BlockSpec((1, ne, tt), lambda i, b: (i, 0, 0)),
            ],
            out_specs=pl.BlockSpec(memory_space=pl.ANY),
            scratch_shapes=[
                pltpu.VMEM((ne, SLOT_ALIGN, d), BF16),
                pltpu.VMEM((ne, SLOT_WIN, d), BF16),
                pltpu.SemaphoreType.DMA((ne,)),
                pltpu.SMEM((ne,), I32),
            ],
        ),
        out_shape=jax.ShapeDtypeStruct((ne, cap_total + pad_rows, d), BF16),
        compiler_params=_params(("arbitrary",)),
        name="moe_dispatch",
    )(base_flat, x2d, slots_t)


def _ffn_kernel(x_ref, wg_ref, wu_ref, wd_ref, o_ref, *, chunk):
    x = x_ref[0]
    ff = wg_ref.shape[2]
    acc = None
    for c in range(ff // chunk):
        cols = slice(c * chunk, (c + 1) * chunk)
        hg = jnp.dot(x, wg_ref[0, :, cols], preferred_element_type=F32)
        hu = jnp.dot(x, wu_ref[0, :, cols], preferred_element_type=F32)
        h = (hg * jax.nn.sigmoid(hg) * hu).astype(BF16)
        part = jnp.dot(h, wd_ref[0, cols, :], preferred_element_type=F32)
        acc = part if acc is None else acc + part
    o_ref[0] = acc.astype(BF16)


def _expert_ffn(xe, wg, wu, wd, cap_total):
    ne, _, d = xe.shape
    ff = wg.shape[2]
    tm = FFN_ROWS
    while cap_total % tm:
        tm //= 2
    chunk = MXU_DIM if ff % MXU_DIM == 0 else LANES
    return pl.pallas_call(
        functools.partial(_ffn_kernel, chunk=chunk),
        grid=(ne, cap_total // tm),
        in_specs=[
            pl.BlockSpec((1, tm, d), lambda e, i: (e, i, 0)),
            pl.BlockSpec((1, d, ff), lambda e, i: (e, 0, 0)),
            pl.BlockSpec((1, d, ff), lambda e, i: (e, 0, 0)),
            pl.BlockSpec((1, ff, d), lambda e, i: (e, 0, 0)),
        ],
        out_specs=pl.BlockSpec((1, tm, d), lambda e, i: (e, i, 0)),
        out_shape=jax.ShapeDtypeStruct((ne, cap_total, d), BF16),
        compiler_params=_params(("parallel", "arbitrary")),
        name="moe_expert_ffn",
    )(xe, wg, wu, wd)


def _combine_kernel(base_ref, *refs, n_experts, cap_total, alpha, split_tile):
    win_refs = refs[:n_experts]
    slot_ref, gate_ref, x_ref, g_ref, beta_ref, ye_ref = refs[n_experts:n_experts + 6]
    out_refs = refs[n_experts + 6:-2]
    extra_ref, sem_ref = refs[-2:]
    i = pl.program_id(0)
    slots = slot_ref[0]
    gates = gate_ref[0]

    def bounds(e, r):
        lower = (base_ref[i * n_experts + e] // SLOT_ALIGN) * SLOT_ALIGN + r * SLOT_WIN
        return lower, jnp.minimum(lower, cap_total - SLOT_WIN)

    def weights(r):
        pieces = []
        for e in range(n_experts):
            lower, start = bounds(e, r)
            srow = slots[e:e + 1, :]
            hit = jnp.logical_and(_onehot_rows(srow, start, SLOT_WIN), srow >= lower)
            pieces.append(jnp.where(hit, gates[e:e + 1, :], 0.0))
        return jnp.concatenate(pieces, axis=0).T.astype(BF16)

    ye0 = jnp.concatenate([r[0] for r in win_refs], axis=0)
    y = jnp.dot(weights(0), ye0, preferred_element_type=F32)

    def extra_round(r, y):
        for e in range(n_experts):
            _, start = bounds(e, r)
            cp = pltpu.make_async_copy(ye_ref.at[e, pl.ds(pl.multiple_of(start, SLOT_ALIGN), SLOT_WIN), :],
                                       extra_ref.at[pl.ds(e * SLOT_WIN, SLOT_WIN), :], sem_ref.at[0])
            cp.start()
            cp.wait()
        return y + jnp.dot(weights(r), extra_ref[...], preferred_element_type=F32)

    y = lax.fori_loop(1, _window_rounds(base_ref, i, n_experts, strict=False), extra_round, y)
    out = _layer_norm(alpha * x_ref[...] + y, g_ref[...], beta_ref[...])
    if split_tile is None:
        out_refs[0][...] = out
    else:
        @pl.when(i < split_tile)
        def _():
            out_refs[0][...] = out

        @pl.when(i >= split_tile)
        def _():
            out_refs[1][...] = out


def _combine_ln(ye, slots_t, gates_t, base_flat, x2d, g, beta, alpha, split_rows=None):
    n, d = x2d.shape
    n_tiles, ne, tt = slots_t.shape
    cap_total = ye.shape[1]
    split_tile = None if split_rows is None else split_rows // tt
    kern = functools.partial(_combine_kernel, n_experts=ne, cap_total=cap_total, alpha=alpha,
                             split_tile=split_tile)

    def win_spec(e):
        def imap(i, b):
            lower = (b[i * ne + e] // SLOT_ALIGN) * SLOT_ALIGN
            return (e, pl.multiple_of(jnp.minimum(lower, cap_total - SLOT_WIN), SLOT_ALIGN), 0)
        return pl.BlockSpec((pl.Element(1), pl.Element(SLOT_WIN), pl.Element(d)), imap)

    in_specs = [win_spec(e) for e in range(ne)]
    in_specs += [
        pl.BlockSpec((1, ne, tt), lambda i, b: (i, 0, 0)),
        pl.BlockSpec((1, ne, tt), lambda i, b: (i, 0, 0)),
        pl.BlockSpec((tt, d), lambda i, b: (i, 0)),
        pl.BlockSpec((1, d), lambda i, b: (0, 0)),
        pl.BlockSpec((1, d), lambda i, b: (0, 0)),
        pl.BlockSpec(memory_space=pl.ANY),
    ]
    if split_tile is None:
        out_specs = [pl.BlockSpec((tt, d), lambda i, b: (i, 0))]
        out_shape = [jax.ShapeDtypeStruct((n, d), F32)]
    else:
        out_specs = [pl.BlockSpec((tt, d), lambda i, b: (jnp.minimum(i, split_tile - 1), 0)),
                     pl.BlockSpec((tt, d), lambda i, b: (jnp.maximum(i - split_tile, 0), 0))]
        out_shape = [jax.ShapeDtypeStruct((split_rows, d), F32),
                     jax.ShapeDtypeStruct((n - split_rows, d), F32)]
    return pl.pallas_call(
        kern,
        grid_spec=pltpu.PrefetchScalarGridSpec(
            num_scalar_prefetch=1,
            grid=(n_tiles,),
            in_specs=in_specs,
            out_specs=out_specs,
            scratch_shapes=[
                pltpu.VMEM((ne * SLOT_WIN, d), BF16),
                pltpu.SemaphoreType.DMA((1,)),
            ],
        ),
        out_shape=out_shape,
        compiler_params=_params(("arbitrary",)),
        name="moe_combine_ln",
    )(base_flat, *([ye] * ne), slots_t, gates_t, x2d, g, beta, ye)


def _filter_kernel(w1t_ref, w1c_ref, w1s_ref, b1_ref, fq_ref, w2_ref, b2_ref, w3_ref, fr_ref, dl_ref,
                   o_ref, *, seq_len):
    half = pl.program_id(0)
    pos = lax.broadcasted_iota(I32, (1, seq_len), 1)
    lag = jnp.where(half == 0, seq_len - pos, pos)
    lagf = lag.astype(F32)
    t = lagf / (seq_len - 1.0)
    w = (2.0 * math.pi) * lagf / seq_len
    fw = fr_ref[...] * w
    hp = lax.Precision.HIGHEST
    pre = (w1t_ref[...] * t
           + jnp.dot(w1c_ref[...], jnp.cos(fw), precision=hp, preferred_element_type=F32)
           - jnp.dot(w1s_ref[...], jnp.sin(fw), precision=hp, preferred_element_type=F32)
           + b1_ref[...])
    fq = fq_ref[...]
    h = jnp.sin(fq * pre)
    h = jnp.sin(fq * (jnp.dot(w2_ref[...], h, precision=hp, preferred_element_type=F32) + b2_ref[...]))
    out = jnp.dot(w3_ref[0], h, precision=hp, preferred_element_type=F32)
    out = out * jnp.exp(-t * dl_ref[...])
    o_ref[...] = jnp.where(lag < seq_len, out, 0.0)


def _hyena_filter(f_w1, f_b1, f_freq, f_w2, f_b2, f_w3, seq_len, d):
    hid = f_w1.shape[1]
    bands = (FILTER_EMB - 1) // 2
    w1 = f_w1.astype(F32).T
    col = lambda v: v.astype(F32).reshape(-1, 1)
    freqs = jnp.linspace(1e-4, bands - 1, bands, dtype=F32).reshape(bands, 1)
    min_decay = math.log(DECAY_TARGET) / SLOW_DECAY_PCT
    max_decay = math.log(DECAY_TARGET) / FAST_DECAY_PCT
    deltas = jnp.abs(jnp.linspace(min_decay, max_decay, d, dtype=F32)).reshape(d, 1)
    w3 = f_w3.astype(F32).T.reshape(2, d, hid)
    full = lambda shape: pl.BlockSpec(shape, lambda s: (0,) * len(shape))
    return pl.pallas_call(
        functools.partial(_filter_kernel, seq_len=seq_len),
        grid=(2,),
        in_specs=[
            full((hid, 1)), full((hid, bands)), full((hid, bands)), full((hid, 1)), full((hid, 1)),
            full((hid, hid)), full((hid, 1)),
            pl.BlockSpec((1, d, hid), lambda s: (1 - s, 0, 0)),
            full((bands, 1)), full((d, 1)),
        ],
        out_specs=pl.BlockSpec((d, seq_len), lambda s: (0, s)),
        out_shape=jax.ShapeDtypeStruct((d, 2 * seq_len), F32),
        compiler_params=_params(("arbitrary",)),
        name="hyena_filter",
    )(w1[:, :1], w1[:, 1:1 + bands], w1[:, 1 + bands:], col(f_b1), col(f_freq),
      f_w2.astype(F32).T, col(f_b2), w3, freqs, deltas)


def _hyena_in_kernel(x_ref, w0_ref, w1_ref, w2_ref, b_ref, cw_ref, cb_ref, x0_ref, z_ref, xb_ref):
    c = pl.program_id(1)

    @pl.when(c == 0)
    def _():
        xb_ref[...] = x_ref[0].astype(BF16)

    xb = xb_ref[...]
    seq_len = xb.shape[0]
    row = lax.broadcasted_iota(I32, (seq_len, 1), 0)
    not_first = row > 0
    not_last = row < seq_len - 1

    def branch(w_ref, k):
        u = jnp.dot(xb, w_ref[...], preferred_element_type=F32) + b_ref[k]
        prev = jnp.where(not_first, pltpu.roll(u, 1, axis=0), 0.0)
        nxt = jnp.where(not_last, pltpu.roll(u, seq_len - 1, axis=0), 0.0)
        cw = cw_ref[k]
        return prev * cw[0:1] + u * cw[1:2] + nxt * cw[2:3] + cb_ref[k]

    x0_ref[0] = branch(w0_ref, 0)
    z_ref[0] = branch(w2_ref, 2) * branch(w1_ref, 1)


def _hyena_in(x3d, w_in, b_in, conv_w, conv_b):
    b, l, d = x3d.shape
    ch = HY_CH
    nc = d // ch
    b3 = b_in.astype(F32).reshape(3, 1, d)
    cw3 = conv_w.astype(F32).reshape(conv_w.shape[0], 3, d).transpose(1, 0, 2)
    cb3 = conv_b.astype(F32).reshape(3, 1, d)
    wspec = lambda k: pl.BlockSpec((d, ch), lambda i, c, k=k: (0, k * nc + c))
    return pl.pallas_call(
        _hyena_in_kernel,
        grid=(b, nc),
        in_specs=[
            pl.BlockSpec((1, l, d), lambda i, c: (i, 0, 0)),
            wspec(0), wspec(1), wspec(2),
            pl.BlockSpec((3, 1, ch), lambda i, c: (0, 0, c)),
            pl.BlockSpec((3, 3, ch), lambda i, c: (0, 0, c)),
            pl.BlockSpec((3, 1, ch), lambda i, c: (0, 0, c)),
        ],
        out_specs=[
            pl.BlockSpec((1, l, ch), lambda i, c: (i, 0, c)),
            pl.BlockSpec((1, l, ch), lambda i, c: (i, 0, c)),
        ],
        out_shape=[
            jax.ShapeDtypeStruct((b, l, d), F32),
            jax.ShapeDtypeStruct((b, l, d), F32),
        ],
        scratch_shapes=[pltpu.VMEM((l, d), BF16)],
        compiler_params=_params(("parallel", "arbitrary")),
        name="hyena_in_conv",
    )(x3d, w_in, w_in, w_in, b3, cw3, cb3)


def _long_conv_kernel(z_ref, x0_ref, kk_ref, skip_ref, o_ref, *, n_blk):
    blk = CONV_BLOCK
    nb = z_ref.shape[1]

    def channel(c, carry):
        zc = z_ref[c]
        zr = jnp.concatenate([zc[:, j * blk:(j + 1) * blk] for j in range(n_blk)], axis=0).astype(BF16)
        kk = kk_ref[pl.ds(c, 1), :]
        acc = [None] * n_blk
        for delta in range(-(n_blk - 1), n_blk):
            start = (n_blk + delta - 1) * blk
            wrow = jnp.broadcast_to(kk[:, start:start + 2 * blk], (blk, 2 * blk))
            rolled = pltpu.roll(wrow, 0, axis=1, stride=1, stride_axis=0)
            tt = rolled[:, blk:].astype(BF16)
            j0, j1 = max(0, -delta), min(n_blk, n_blk - delta)
            res = jnp.dot(zr[j0 * nb:j1 * nb, :], tt, preferred_element_type=F32)
            for j in range(j0, j1):
                part = res[(j - j0) * nb:(j - j0 + 1) * nb, :]
                i = j + delta
                acc[i] = part if acc[i] is None else acc[i] + part
        y = jnp.concatenate(acc, axis=1)
        o_ref[c] = (y + zc * skip_ref[pl.ds(c, 1), :]) * x0_ref[c]
        return carry

    lax.fori_loop(0, z_ref.shape[0], channel, 0)


def _long_conv(z_t, x0_t, kk, skip):
    d, b, l = z_t.shape
    ch = CONV_CH
    kern = functools.partial(_long_conv_kernel, n_blk=l // CONV_BLOCK)
    return pl.pallas_call(
        kern,
        grid=(d // ch,),
        in_specs=[
            pl.BlockSpec((ch, b, l), lambda i: (i, 0, 0)),
            pl.BlockSpec((ch, b, l), lambda i: (i, 0, 0)),
            pl.BlockSpec((ch, 2 * l), lambda i: (i, 0)),
            pl.BlockSpec((ch, 1), lambda i: (i, 0)),
        ],
        out_specs=pl.BlockSpec((ch, b, l), lambda i: (i, 0, 0)),
        out_shape=jax.ShapeDtypeStruct((d, b, l), F32),
        compiler_params=_params(("parallel",)),
        name="hyena_long_conv",
    )(z_t, x0_t, kk, skip)


def _rope_tables(seq_len, head_dim):
    rows = seq_len // GRID_W
    row = jnp.repeat(jnp.arange(rows), GRID_W)
    col = jnp.tile(jnp.arange(GRID_W), rows)
    axis_dim = head_dim // 2
    inv = ROPE_THETA ** (-jnp.arange(0, axis_dim, 2, dtype=F32) / axis_dim)
    ang = jnp.concatenate([row[:, None] * inv, col[:, None] * inv], -1)
    cos, sin = jnp.cos(ang), jnp.sin(ang)
    reps = LANES // head_dim
    cc = jnp.tile(jnp.concatenate([cos, cos], -1), (1, reps))
    ss = jnp.tile(jnp.concatenate([-sin, sin], -1), (1, reps))
    return cc, ss


def _moe(x1, aff, group_tokens, w_gate, w_up, w_down, g, beta, alpha, split_rows=None):
    n, d = x1.shape
    ne = aff.shape[1]
    tt = TOK_TILE
    slots, gates, bases = [], [], []
    tok0, slot0 = 0, 0
    for ng in group_tokens:
        cap = EC_CAPACITY * ng // ne
        s, gt, bs = _select(aff[tok0:tok0 + ng].T.reshape(ne, ng // tt, tt), cap, slot0)
        slots.append(s)
        gates.append(gt)
        bases.append(bs[:, :, 0])
        tok0 += ng
        slot0 += cap
    cap_total = slot0
    slots_t = jnp.concatenate(slots, axis=1).transpose(1, 0, 2)
    gates_t = jnp.concatenate(gates, axis=1).transpose(1, 0, 2)
    base = jnp.concatenate(bases + [jnp.full((ne, 1), cap_total, I32)], axis=1).T
    base_flat = base.reshape(-1)
    xe = _dispatch(x1, slots_t, base_flat, cap_total)
    ye = _expert_ffn(xe, w_gate, w_up, w_down, cap_total)
    return _combine_ln(ye, slots_t, gates_t, base_flat, x1, g, beta, alpha, split_rows)


def kernel(x_prompt, x_sample, attn_w_qkv, attn_q_gain, attn_k_gain, attn_w_o, hy_w_in, hy_b_in, hy_conv_w, hy_conv_b, hy_f_w1, hy_f_b1, hy_f_freq, hy_f_w2, hy_f_b2, hy_f_w3, hy_skip, hy_w_out, hy_b_out, ln_mix_g, ln_mix_b, moe_router, moe_w_gate, moe_w_up, moe_w_down, ln_ffn_g, ln_ffn_b):
    bp, l, d = x_prompt.shape
    bs = x_sample.shape[0]
    assert x_sample.shape[1] == l
    depth = ln_mix_g.shape[0]
    alpha = (2 * depth) ** 0.25
    head_dim = d // N_HEADS
    b = bp + bs
    n = b * l
    group_tokens = (bp * l, bs * l)

    x = jnp.concatenate([x_prompt, x_sample], axis=0).reshape(n, d)
    cc, ss = _rope_tables(l, head_dim)
    seg = (lax.broadcasted_iota(I32, (LANES, LANES), 0) // head_dim
           == lax.broadcasted_iota(I32, (LANES, LANES), 1) // head_dim).astype(BF16)
    row = lambda v: v.astype(F32).reshape(1, -1)
    zero_bias = jnp.zeros((1, d), F32)

    for i in range(depth):
        j = i // 2
        r_hi, r_lo = _split_bf16(moe_router[i].astype(F32))
        if i % 2 == 0:
            gq = jnp.tile(row(attn_q_gain[j]), (1, LANES // head_dim))
            gk = jnp.tile(row(attn_k_gain[j]), (1, LANES // head_dim))
            q, k, v = _qkv_rope(x, attn_w_qkv[j].astype(BF16), cc, ss, gq, gk, seg, l)
            o = _attention(q.reshape(b, l, -1), k.reshape(b, l, -1), v.reshape(b, l, -1), tq=min(256, l))
            mix_in, w_mix, b_mix = o.reshape(n, -1), attn_w_o[j].astype(BF16), zero_bias
        else:
            kk = _hyena_filter(hy_f_w1[j], hy_f_b1[j], hy_f_freq[j], hy_f_w2[j], hy_f_b2[j], hy_f_w3[j], l, d)
            x0, z = _hyena_in(x.reshape(b, l, d), hy_w_in[j].astype(BF16), hy_b_in[j], hy_conv_w[j], hy_conv_b[j])
            y_t = _long_conv(z.transpose(2, 0, 1), x0.transpose(2, 0, 1), kk,
                             hy_skip[j].astype(F32).reshape(d, 1))
            mix_in = y_t.transpose(1, 2, 0).astype(BF16).reshape(n, d)
            w_mix, b_mix = hy_w_out[j].astype(BF16), row(hy_b_out[j])
        x1, aff = _proj_ln_router(mix_in, w_mix, b_mix, x, row(ln_mix_g[i]), row(ln_mix_b[i]),
                                  r_hi, r_lo, alpha)
        last = i == depth - 1
        out = _moe(x1, aff, group_tokens, moe_w_gate[i].astype(BF16), moe_w_up[i].astype(BF16),
                   moe_w_down[i].astype(BF16), row(ln_ffn_g[i]), row(ln_ffn_b[i]), alpha,
                   split_rows=group_tokens[0] if last else None)
        if last:
            return (out[0].reshape(bp, l, d), out[1].reshape(bs, l, d))
        x = out[0]
```

```python
import functools
import math

import jax
import jax.numpy as jnp
from jax import lax
from jax.experimental import pallas as pl
from jax.experimental.pallas import tpu as pltpu

F32 = jnp.float32
BF16 = jnp.bfloat16
I32 = jnp.int32

N_HEADS = 16
N_KV_HEADS = 4
GRID_W = 64
ROPE_THETA = 10000.0
QK_EPS = 1e-6
FILTER_EMB = 33
DECAY_TARGET = 1e-2
FAST_DECAY_PCT = 0.3
SLOW_DECAY_PCT = 1.5
EC_CAPACITY = 2
LN_EPS = 1e-5

LANES = 128
MXU_DIM = 256
VMEM_LIMIT_BYTES = 56 * 1024 * 1024

ROW_TILE = 512
PROJ_ROWS = 256
TOK_TILE = 256
SLOT_ALIGN = 16
SLOT_WIN = 64
FFN_ROWS = 1024
CONV_BLOCK = 256
CONV_CH = 8
HY_CH = 256


def _params(sem, vmem=VMEM_LIMIT_BYTES):
    return pltpu.CompilerParams(dimension_semantics=sem, vmem_limit_bytes=vmem)


def _layer_norm(v, g, b):
    mu = jnp.mean(v, axis=-1, keepdims=True)
    c = v - mu
    var = jnp.mean(c * c, axis=-1, keepdims=True)
    return c * lax.rsqrt(var + LN_EPS) * g + b


def _split_bf16(v):
    hi = v.astype(BF16)
    lo = (v - hi.astype(F32)).astype(BF16)
    return hi, lo


def _qkv_kernel(x_ref, w_ref, cc_ref, ss_ref, gq_ref, gk_ref, seg_ref, q_ref, k_ref, v_ref, *,
                n_q_chunks, n_k_chunks, head_dim, scale):
    xb = x_ref[...].astype(BF16)
    qkv = jnp.dot(xb, w_ref[...], preferred_element_type=F32)
    cc = cc_ref[...]
    ss = ss_ref[...]
    seg = seg_ref[...]
    lane = lax.broadcasted_iota(I32, (xb.shape[0], LANES), 1)
    first_half = (lane % head_dim) < (head_dim // 2)
    half = head_dim // 2
    for c in range(n_q_chunks + n_k_chunks):
        u = qkv[:, c * LANES:(c + 1) * LANES]
        s_hi, s_lo = _split_bf16(u * u)
        ssum = (jnp.dot(s_hi, seg, preferred_element_type=F32)
                + jnp.dot(s_lo, seg, preferred_element_type=F32))
        r = lax.rsqrt(ssum * (1.0 / head_dim) + QK_EPS)
        gain = gq_ref[...] if c < n_q_chunks else gk_ref[...]
        un = u * r * gain
        partner = jnp.where(first_half, pltpu.roll(un, LANES - half, axis=1), pltpu.roll(un, half, axis=1))
        o = un * cc + partner * ss
        if c < n_q_chunks:
            q_ref[:, c * LANES:(c + 1) * LANES] = (o * scale).astype(BF16)
        else:
            ck = c - n_q_chunks
            k_ref[:, ck * LANES:(ck + 1) * LANES] = o.astype(BF16)
    v_ref[...] = qkv[:, (n_q_chunks + n_k_chunks) * LANES:].astype(BF16)


def _qkv_rope(x2d, w_qkv, cc, ss, gq, gk, seg, seq_len):
    n, d = x2d.shape
    head_dim = d // N_HEADS
    dq = N_HEADS * head_dim
    dk = N_KV_HEADS * head_dim
    tm = ROW_TILE
    tiles_per_seq = seq_len // tm
    kern = functools.partial(_qkv_kernel, n_q_chunks=dq // LANES, n_k_chunks=dk // LANES,
                             head_dim=head_dim, scale=head_dim ** -0.5)
    return pl.pallas_call(
        kern,
        grid=(n // tm,),
        in_specs=[
            pl.BlockSpec((tm, d), lambda i: (i, 0)),
            pl.BlockSpec((d, dq + 2 * dk), lambda i: (0, 0)),
            pl.BlockSpec((tm, LANES), lambda i: (i % tiles_per_seq, 0)),
            pl.BlockSpec((tm, LANES), lambda i: (i % tiles_per_seq, 0)),
            pl.BlockSpec((1, LANES), lambda i: (0, 0)),
            pl.BlockSpec((1, LANES), lambda i: (0, 0)),
            pl.BlockSpec((LANES, LANES), lambda i: (0, 0)),
        ],
        out_specs=[
            pl.BlockSpec((tm, dq), lambda i: (i, 0)),
            pl.BlockSpec((tm, dk), lambda i: (i, 0)),
            pl.BlockSpec((tm, dk), lambda i: (i, 0)),
        ],
        out_shape=[
            jax.ShapeDtypeStruct((n, dq), BF16),
            jax.ShapeDtypeStruct((n, dk), BF16),
            jax.ShapeDtypeStruct((n, dk), BF16),
        ],
        compiler_params=_params(("parallel",)),
        name="attn_qkv_rope",
    )(x2d, w_qkv, cc, ss, gq, gk, seg)


def _attn_kernel(q_ref, k_ref, v_ref, o_ref, *, head_dim, group):
    tq = q_ref.shape[1]
    for kv in range(N_KV_HEADS):
        k = k_ref[0, :, kv * head_dim:(kv + 1) * head_dim]
        v = v_ref[0, :, kv * head_dim:(kv + 1) * head_dim]
        qs = [q_ref[0, :, (kv * group + g) * head_dim:(kv * group + g + 1) * head_dim] for g in range(group)]
        q = jnp.concatenate(qs, axis=0)
        s = lax.dot_general(q, k, (((1,), (1,)), ((), ())), preferred_element_type=F32)
        m = jnp.max(s, axis=-1, keepdims=True)
        p = jnp.exp(s - m)
        l = jnp.sum(p, axis=-1, keepdims=True)
        o = jnp.dot(p.astype(BF16), v, preferred_element_type=F32) / l
        outs = [o[g * tq:(g + 1) * tq, :] for g in range(group)]
        o_ref[0, :, kv * group * head_dim:(kv + 1) * group * head_dim] = (
            jnp.concatenate(outs, axis=1).astype(BF16))


def _attention(q, k, v, tq):
    b, l, dq = q.shape
    dk = k.shape[2]
    head_dim = dq // N_HEADS
    kern = functools.partial(_attn_kernel, head_dim=head_dim, group=N_HEADS // N_KV_HEADS)
    return pl.pallas_call(
        kern,
        grid=(b, l // tq),
        in_specs=[
            pl.BlockSpec((1, tq, dq), lambda i, j: (i, j, 0)),
            pl.BlockSpec((1, l, dk), lambda i, j: (i, 0, 0)),
            pl.BlockSpec((1, l, dk), lambda i, j: (i, 0, 0)),
        ],
        out_specs=pl.BlockSpec((1, tq, dq), lambda i, j: (i, j, 0)),
        out_shape=jax.ShapeDtypeStruct((b, l, dq), BF16),
        compiler_params=_params(("parallel", "parallel")),
        name="attn_core",
    )(q, k, v)


def _proj_ln_router_kernel(m_ref, w_ref, b_ref, x_ref, g_ref, beta_ref, rc_ref, rh_ref,
                           y_ref, aff_ref, *, alpha):
    mix = jnp.dot(m_ref[...], w_ref[...], preferred_element_type=F32) + b_ref[...]
    y = _layer_norm(alpha * x_ref[...] + mix, g_ref[...], beta_ref[...])
    y_ref[...] = y
    y_hi, y_lo = _split_bf16(y)
    ne = rh_ref.shape[1]
    both = jnp.dot(y_hi, rc_ref[...], preferred_element_type=F32)
    lg = both[:, :ne] + both[:, ne:] + jnp.dot(y_lo, rh_ref[...], preferred_element_type=F32)
    ex = jnp.exp(lg - jnp.max(lg, axis=-1, keepdims=True))
    aff_ref[...] = ex / jnp.sum(ex, axis=-1, keepdims=True)


def _proj_ln_router(mix_in, w, bias, x2d, g, beta, r_hi, r_lo, alpha):
    n, d = x2d.shape
    dm = mix_in.shape[1]
    ne = r_hi.shape[1]
    tm = PROJ_ROWS
    kern = functools.partial(_proj_ln_router_kernel, alpha=alpha)
    return pl.pallas_call(
        kern,
        grid=(n // tm,),
        in_specs=[
            pl.BlockSpec((tm, dm), lambda i: (i, 0)),
            pl.BlockSpec((dm, d), lambda i: (0, 0)),
            pl.BlockSpec((1, d), lambda i: (0, 0)),
            pl.BlockSpec((tm, d), lambda i: (i, 0)),
            pl.BlockSpec((1, d), lambda i: (0, 0)),
            pl.BlockSpec((1, d), lambda i: (0, 0)),
            pl.BlockSpec((d, 2 * ne), lambda i: (0, 0)),
            pl.BlockSpec((d, ne), lambda i: (0, 0)),
        ],
        out_specs=[
            pl.BlockSpec((tm, d), lambda i: (i, 0)),
            pl.BlockSpec((tm, ne), lambda i: (i, 0)),
        ],
        out_shape=[
            jax.ShapeDtypeStruct((n, d), F32),
            jax.ShapeDtypeStruct((n, ne), F32),
        ],
        compiler_params=_params(("parallel",)),
        name="proj_ln_router",
    )(mix_in, w, bias, x2d, g, beta, jnp.concatenate([r_hi, r_lo], axis=1), r_hi)


def _select_kernel(aff_ref, tri_ref, low_ref, slot_ref, gate_ref, base_ref, *, cap, slot_offset):
    aff = aff_ref[0]
    nt, tt = aff.shape
    bits = pltpu.bitcast(aff, I32)

    def count(mask):
        c = jnp.sum(mask.astype(F32), axis=1, keepdims=True)
        return jnp.sum(c, axis=0, keepdims=True)

    def step(i, prefix):
        cand = prefix | jnp.left_shift(jnp.int32(1), 30 - i)
        return jnp.where(count(bits >= cand) >= cap, cand, prefix)

    thr = lax.fori_loop(0, 31, step, jnp.zeros((1, 1), I32))
    gt = bits > thr
    eq = bits == thr
    need = cap - count(gt)

    tri = tri_ref[...]
    low = low_ref[...]
    ones = jnp.ones((tt, LANES), BF16)

    def excl_cumsum(mask):
        mf = mask.astype(F32).astype(BF16)
        within = jnp.dot(mf, tri, preferred_element_type=F32)
        tot = jnp.dot(mf, ones, preferred_element_type=F32).astype(BF16)
        base = jnp.dot(low, tot, preferred_element_type=F32)
        return within + base[:, :1], base

    rank_eq, _ = excl_cumsum(eq)
    sel = gt | (eq & (rank_eq < need))
    pos, base = excl_cumsum(sel)
    slot_ref[0] = jnp.where(sel, pos.astype(I32) + slot_offset, -1)
    gate_ref[0] = jnp.where(sel, aff, 0.0)
    base_ref[0] = base.astype(I32) + slot_offset


def _select(aff_t, cap, slot_offset):
    ne, nt, tt = aff_t.shape
    tri = (lax.broadcasted_iota(I32, (tt, tt), 0) < lax.broadcasted_iota(I32, (tt, tt), 1)).astype(BF16)
    low = (lax.broadcasted_iota(I32, (nt, nt), 0) > lax.broadcasted_iota(I32, (nt, nt), 1)).astype(BF16)
    kern = functools.partial(_select_kernel, cap=cap, slot_offset=slot_offset)
    return pl.pallas_call(
        kern,
        grid=(ne,),
        in_specs=[
            pl.BlockSpec((1, nt, tt), lambda e: (e, 0, 0)),
            pl.BlockSpec((tt, tt), lambda e: (0, 0)),
            pl.BlockSpec((nt, nt), lambda e: (0, 0)),
        ],
        out_specs=[
            pl.BlockSpec((1, nt, tt), lambda e: (e, 0, 0)),
            pl.BlockSpec((1, nt, tt), lambda e: (e, 0, 0)),
            pl.BlockSpec((1, nt, LANES), lambda e: (e, 0, 0)),
        ],
        out_shape=[
            jax.ShapeDtypeStruct((ne, nt, tt), I32),
            jax.ShapeDtypeStruct((ne, nt, tt), F32),
            jax.ShapeDtypeStruct((ne, nt, LANES), I32),
        ],
        compiler_params=_params(("parallel",)),
        name="moe_select",
    )(aff_t, tri, low)


def _onehot_rows(slot_row, first_slot, n_rows):
    tt = slot_row.shape[1]
    rows = lax.broadcasted_iota(I32, (n_rows, tt), 0) + first_slot
    return rows == slot_row


def _window_rounds(base_ref, i, n_experts, strict):
    n_rounds = jnp.int32(1)
    for e in range(n_experts):
        b0 = base_ref[i * n_experts + e]
        b1 = base_ref[(i + 1) * n_experts + e]
        span = b1 - (b0 // SLOT_ALIGN) * SLOT_ALIGN
        r = span // SLOT_WIN + 1 if strict else (span + SLOT_WIN - 1) // SLOT_WIN
        n_rounds = jnp.maximum(n_rounds, r)
    return n_rounds


def _dispatch_kernel(base_ref, x_ref, slot_ref, xe_ref, carry_ref, stage_ref, sem_ref, last_ref, *,
                     n_tiles, n_experts, cap_total, pad_windows):
    i = pl.program_id(0)

    def window_copy(e, start):
        rows = pl.ds(pl.multiple_of(start, SLOT_ALIGN), SLOT_WIN)
        return pltpu.make_async_copy(stage_ref.at[e], xe_ref.at[e, rows, :], sem_ref.at[e])

    @pl.when(i == 0)
    def _():
        carry_ref[...] = jnp.zeros_like(carry_ref)
        stage_ref[...] = jnp.zeros_like(stage_ref)
        pads = [window_copy(e, cap_total + k * SLOT_WIN) for e in range(n_experts) for k in range(pad_windows)]
        for cp in pads:
            cp.start()
        for cp in pads:
            cp.wait()

    xb = x_ref[...].astype(BF16)
    slots = slot_ref[0]

    def do_round(r, carry):
        starts = [(base_ref[i * n_experts + e] // SLOT_ALIGN) * SLOT_ALIGN + r * SLOT_WIN
                  for e in range(n_experts)]
        onehot = jnp.concatenate(
            [_onehot_rows(slots[e:e + 1, :], starts[e], SLOT_WIN) for e in range(n_experts)], axis=0)
        contrib = jnp.dot(onehot.astype(F32).astype(BF16), xb, preferred_element_type=F32)
        for e in range(n_experts):
            data = contrib[e * SLOT_WIN:(e + 1) * SLOT_WIN, :].astype(BF16)
            head = jnp.where(r == 0, carry_ref[e], jnp.zeros_like(carry_ref[e]))

            @pl.when(jnp.logical_or(i > 0, r > 0))
            def _(e=e):
                window_copy(e, last_ref[e]).wait()

            stage_ref[e, :SLOT_ALIGN, :] = data[:SLOT_ALIGN, :] + head
            stage_ref[e, SLOT_ALIGN:, :] = data[SLOT_ALIGN:, :]
            window_copy(e, starts[e]).start()
            last_ref[e] = starts[e]
            rem = base_ref[(i + 1) * n_experts + e] - starts[e]
            k = jnp.clip(rem // SLOT_ALIGN, 0, SLOT_WIN // SLOT_ALIGN - 1)
            tail = stage_ref[e, pl.ds(pl.multiple_of(k * SLOT_ALIGN, SLOT_ALIGN), SLOT_ALIGN), :]
            is_last = jnp.logical_and(rem >= 0, rem < SLOT_WIN)
            carry_ref[e] = jnp.where(is_last, tail, carry_ref[e])
        return carry

    lax.fori_loop(0, _window_rounds(base_ref, i, n_experts, strict=True), do_round, 0)

    @pl.when(i == n_tiles - 1)
    def _():
        for e in range(n_experts):
            window_copy(e, last_ref[e]).wait()


def _dispatch(x2d, slots_t, base_flat, cap_total):
    n, d = x2d.shape
    n_tiles, ne, tt = slots_t.shape
    pad_windows = tt // SLOT_WIN + 1
    pad_rows = pad_windows * SLOT_WIN
    kern = functools.partial(_dispatch_kernel, n_tiles=n_tiles, n_experts=ne, cap_total=cap_total,
                             pad_windows=pad_windows)
    return pl.pallas_call(
        kern,
        grid_spec=pltpu.PrefetchScalarGridSpec(
            num_scalar_prefetch=1,
            grid=(n_tiles,),
            in_specs=[
                pl.BlockSpec((tt, d), lambda i, b: (i, 0)),
                pl.BlockSpec((1, ne, tt), lambda i, b: (i, 0, 0)),
            ],
            out_specs=pl.BlockSpec(memory_space=pl.ANY),
            scratch_shapes=[
                pltpu.VMEM((ne, SLOT_ALIGN, d), BF16),
                pltpu.VMEM((ne, SLOT_WIN, d), BF16),
                pltpu.SemaphoreType.DMA((ne,)),
                pltpu.SMEM((ne,), I32),
            ],
        ),
        out_shape=jax.ShapeDtypeStruct((ne, cap_total + pad_rows, d), BF16),
        compiler_params=_params(("arbitrary",)),
        name="moe_dispatch",
    )(base_flat, x2d, slots_t)


def _ffn_kernel(x_ref, wg_ref, wu_ref, wd_ref, o_ref, *, chunk):
    x = x_ref[0]
    ff = wg_ref.shape[2]
    acc = None
    for c in range(ff // chunk):
        cols = slice(c * chunk, (c + 1) * chunk)
        hg = jnp.dot(x, wg_ref[0, :, cols], preferred_element_type=F32)
        hu = jnp.dot(x, wu_ref[0, :, cols], preferred_element_type=F32)
        h = (hg * jax.nn.sigmoid(hg) * hu).astype(BF16)
        part = jnp.dot(h, wd_ref[0, cols, :], preferred_element_type=F32)
        acc = part if acc is None else acc + part
    o_ref[0] = acc.astype(BF16)


def _expert_ffn(xe, wg, wu, wd, cap_total, layer):
    ne, _, d = xe.shape
    ff = wg.shape[2]
    tm = FFN_ROWS
    while cap_total % tm:
        tm //= 2
    chunk = MXU_DIM if ff % MXU_DIM == 0 else LANES
    w0 = layer * ne
    return pl.pallas_call(
        functools.partial(_ffn_kernel, chunk=chunk),
        grid=(ne, cap_total // tm),
        in_specs=[
            pl.BlockSpec((1, tm, d), lambda e, i: (e, i, 0)),
            pl.BlockSpec((1, d, ff), lambda e, i: (w0 + e, 0, 0)),
            pl.BlockSpec((1, d, ff), lambda e, i: (w0 + e, 0, 0)),
            pl.BlockSpec((1, ff, d), lambda e, i: (w0 + e, 0, 0)),
        ],
        out_specs=pl.BlockSpec((1, tm, d), lambda e, i: (e, i, 0)),
        out_shape=jax.ShapeDtypeStruct((ne, cap_total, d), BF16),
        compiler_params=_params(("parallel", "arbitrary")),
        name="moe_expert_ffn",
    )(xe, wg, wu, wd)


def _combine_kernel(base_ref, *refs, n_experts, cap_total, alpha, split_tile):
    win_refs = refs[:n_experts]
    slot_ref, gate_ref, x_ref, g_ref, beta_ref, ye_ref = refs[n_experts:n_experts + 6]
    out_refs = refs[n_experts + 6:-2]
    extra_ref, sem_ref = refs[-2:]
    i = pl.program_id(0)
    slots = slot_ref[0]
    gates = gate_ref[0]

    def bounds(e, r):
        lower = (base_ref[i * n_experts + e] // SLOT_ALIGN) * SLOT_ALIGN + r * SLOT_WIN
        return lower, jnp.minimum(lower, cap_total - SLOT_WIN)

    def weights(r):
        pieces = []
        for e in range(n_experts):
            lower, start = bounds(e, r)
            srow = slots[e:e + 1, :]
            hit = jnp.logical_and(_onehot_rows(srow, start, SLOT_WIN), srow >= lower)
            pieces.append(jnp.where(hit, gates[e:e + 1, :], 0.0))
        return jnp.concatenate(pieces, axis=0).T.astype(BF16)

    ye0 = jnp.concatenate([r[0] for r in win_refs], axis=0)
    y = jnp.dot(weights(0), ye0, preferred_element_type=F32)

    def extra_round(r, y):
        for e in range(n_experts):
            _, start = bounds(e, r)
            cp = pltpu.make_async_copy(ye_ref.at[e, pl.ds(pl.multiple_of(start, SLOT_ALIGN), SLOT_WIN), :],
                                       extra_ref.at[pl.ds(e * SLOT_WIN, SLOT_WIN), :], sem_ref.at[0])
            cp.start()
            cp.wait()
        return y + jnp.dot(weights(r), extra_ref[...], preferred_element_type=F32)

    y = lax.fori_loop(1, _window_rounds(base_ref, i, n_experts, strict=False), extra_round, y)
    out = _layer_norm(alpha * x_ref[...] + y, g_ref[...], beta_ref[...])
    if split_tile is None:
        out_refs[0][...] = out
    else:
        @pl.when(i < split_tile)
        def _():
            out_refs[0][...] = out

        @pl.when(i >= split_tile)
        def _():
            out_refs[1][...] = out


def _combine_ln(ye, slots_t, gates_t, base_flat, x2d, g, beta, alpha, split_rows=None):
    n, d = x2d.shape
    n_tiles, ne, tt = slots_t.shape
    cap_total = ye.shape[1]
    split_tile = None if split_rows is None else split_rows // tt
    kern = functools.partial(_combine_kernel, n_experts=ne, cap_total=cap_total, alpha=alpha,
                             split_tile=split_tile)

    def win_spec(e):
        def imap(i, b):
            lower = (b[i * ne + e] // SLOT_ALIGN) * SLOT_ALIGN
            return (e, pl.multiple_of(jnp.minimum(lower, cap_total - SLOT_WIN), SLOT_ALIGN), 0)
        return pl.BlockSpec((pl.Element(1), pl.Element(SLOT_WIN), pl.Element(d)), imap)

    in_specs = [win_spec(e) for e in range(ne)]
    in_specs += [
        pl.BlockSpec((1, ne, tt), lambda i, b: (i, 0, 0)),
        pl.BlockSpec((1, ne, tt), lambda i, b: (i, 0, 0)),
        pl.BlockSpec((tt, d), lambda i, b: (i, 0)),
        pl.BlockSpec((1, d), lambda i, b: (0, 0)),
        pl.BlockSpec((1, d), lambda i, b: (0, 0)),
        pl.BlockSpec(memory_space=pl.ANY),
    ]
    if split_tile is None:
        out_specs = [pl.BlockSpec((tt, d), lambda i, b: (i, 0))]
        out_shape = [jax.ShapeDtypeStruct((n, d), F32)]
    else:
        out_specs = [pl.BlockSpec((tt, d), lambda i, b: (jnp.minimum(i, split_tile - 1), 0)),
                     pl.BlockSpec((tt, d), lambda i, b: (jnp.maximum(i - split_tile, 0), 0))]
        out_shape = [jax.ShapeDtypeStruct((split_rows, d), F32),
                     jax.ShapeDtypeStruct((n - split_rows, d), F32)]
    return pl.pallas_call(
        kern,
        grid_spec=pltpu.PrefetchScalarGridSpec(
            num_scalar_prefetch=1,
            grid=(n_tiles,),
            in_specs=in_specs,
            out_specs=out_specs,
            scratch_shapes=[
                pltpu.VMEM((ne * SLOT_WIN, d), BF16),
                pltpu.SemaphoreType.DMA((1,)),
            ],
        ),
        out_shape=out_shape,
        compiler_params=_params(("arbitrary",)),
        name="moe_combine_ln",
    )(base_flat, *([ye] * ne), slots_t, gates_t, x2d, g, beta, ye)


def _filter_kernel(w1t_ref, w1c_ref, w1s_ref, b1_ref, fq_ref, w2_ref, b2_ref, w3_ref, fr_ref, dl_ref,
                   o_ref, *, seq_len):
    half = pl.program_id(0)
    pos = lax.broadcasted_iota(I32, (1, seq_len), 1)
    lag = jnp.where(half == 0, seq_len - pos, pos)
    lagf = lag.astype(F32)
    t = lagf / (seq_len - 1.0)
    w = (2.0 * math.pi) * lagf / seq_len
    fw = fr_ref[...] * w
    hp = lax.Precision.HIGHEST
    pre = (w1t_ref[...] * t
           + jnp.dot(w1c_ref[...], jnp.cos(fw), precision=hp, preferred_element_type=F32)
           - jnp.dot(w1s_ref[...], jnp.sin(fw), precision=hp, preferred_element_type=F32)
           + b1_ref[...])
    fq = fq_ref[...]
    h = jnp.sin(fq * pre)
    h = jnp.sin(fq * (jnp.dot(w2_ref[...], h, precision=hp, preferred_element_type=F32) + b2_ref[...]))
    out = jnp.dot(w3_ref[0], h, precision=hp, preferred_element_type=F32)
    out = out * jnp.exp(-t * dl_ref[...])
    o_ref[...] = jnp.where(lag < seq_len, out, 0.0)


def _hyena_filter(f_w1, f_b1, f_freq, f_w2, f_b2, f_w3, seq_len, d):
    hid = f_w1.shape[1]
    bands = (FILTER_EMB - 1) // 2
    w1 = f_w1.astype(F32).T
    col = lambda v: v.astype(F32).reshape(-1, 1)
    freqs = jnp.linspace(1e-4, bands - 1, bands, dtype=F32).reshape(bands, 1)
    min_decay = math.log(DECAY_TARGET) / SLOW_DECAY_PCT
    max_decay = math.log(DECAY_TARGET) / FAST_DECAY_PCT
    deltas = jnp.abs(jnp.linspace(min_decay, max_decay, d, dtype=F32)).reshape(d, 1)
    w3 = f_w3.astype(F32).T.reshape(2, d, hid)
    full = lambda shape: pl.BlockSpec(shape, lambda s: (0,) * len(shape))
    return pl.pallas_call(
        functools.partial(_filter_kernel, seq_len=seq_len),
        grid=(2,),
        in_specs=[
            full((hid, 1)), full((hid, bands)), full((hid, bands)), full((hid, 1)), full((hid, 1)),
            full((hid, hid)), full((hid, 1)),
            pl.BlockSpec((1, d, hid), lambda s: (1 - s, 0, 0)),
            full((bands, 1)), full((d, 1)),
        ],
        out_specs=pl.BlockSpec((d, seq_len), lambda s: (0, s)),
        out_shape=jax.ShapeDtypeStruct((d, 2 * seq_len), F32),
        compiler_params=_params(("arbitrary",)),
        name="hyena_filter",
    )(w1[:, :1], w1[:, 1:1 + bands], w1[:, 1 + bands:], col(f_b1), col(f_freq),
      f_w2.astype(F32).T, col(f_b2), w3, freqs, deltas)


def _hyena_in_kernel(x_ref, w0_ref, w1_ref, w2_ref, b_ref, cw_ref, cb_ref, x0_ref, z_ref, xb_ref):
    c = pl.program_id(1)

    @pl.when(c == 0)
    def _():
        xb_ref[...] = x_ref[0].astype(BF16)

    xb = xb_ref[...]
    seq_len = xb.shape[0]
    row = lax.broadcasted_iota(I32, (seq_len, 1), 0)
    not_first = row > 0
    not_last = row < seq_len - 1

    def branch(w_ref, k):
        u = jnp.dot(xb, w_ref[...], preferred_element_type=F32) + b_ref[k]
        prev = jnp.where(not_first, pltpu.roll(u, 1, axis=0), 0.0)
        nxt = jnp.where(not_last, pltpu.roll(u, seq_len - 1, axis=0), 0.0)
        cw = cw_ref[k]
        return prev * cw[0:1] + u * cw[1:2] + nxt * cw[2:3] + cb_ref[k]

    x0_ref[0] = branch(w0_ref, 0)
    z_ref[0] = branch(w2_ref, 2) * branch(w1_ref, 1)


def _hyena_in(x3d, w_in, b_in, conv_w, conv_b):
    b, l, d = x3d.shape
    ch = HY_CH
    nc = d // ch
    b3 = b_in.astype(F32).reshape(3, 1, d)
    cw3 = conv_w.astype(F32).reshape(conv_w.shape[0], 3, d).transpose(1, 0, 2)
    cb3 = conv_b.astype(F32).reshape(3, 1, d)
    wspec = lambda k: pl.BlockSpec((d, ch), lambda i, c, k=k: (0, k * nc + c))
    return pl.pallas_call(
        _hyena_in_kernel,
        grid=(b, nc),
        in_specs=[
            pl.BlockSpec((1, l, d), lambda i, c: (i, 0, 0)),
            wspec(0), wspec(1), wspec(2),
            pl.BlockSpec((3, 1, ch), lambda i, c: (0, 0, c)),
            pl.BlockSpec((3, 3, ch), lambda i, c: (0, 0, c)),
            pl.BlockSpec((3, 1, ch), lambda i, c: (0, 0, c)),
        ],
        out_specs=[
            pl.BlockSpec((1, l, ch), lambda i, c: (i, 0, c)),
            pl.BlockSpec((1, l, ch), lambda i, c: (i, 0, c)),
        ],
        out_shape=[
            jax.ShapeDtypeStruct((b, l, d), F32),
            jax.ShapeDtypeStruct((b, l, d), F32),
        ],
        scratch_shapes=[pltpu.VMEM((l, d), BF16)],
        compiler_params=_params(("parallel", "arbitrary")),
        name="hyena_in_conv",
    )(x3d, w_in, w_in, w_in, b3, cw3, cb3)


def _long_conv_kernel(z_ref, x0_ref, kk_ref, skip_ref, o_ref, *, n_blk):
    blk = CONV_BLOCK
    nb = z_ref.shape[1]

    def channel(c, carry):
        zc = z_ref[c]
        zr = jnp.concatenate([zc[:, j * blk:(j + 1) * blk] for j in range(n_blk)], axis=0).astype(BF16)
        kk = kk_ref[pl.ds(c, 1), :]
        acc = [None] * n_blk
        for delta in range(-(n_blk - 1), n_blk):
            start = (n_blk + delta - 1) * blk
            wrow = jnp.broadcast_to(kk[:, start:start + 2 * blk], (blk, 2 * blk))
            rolled = pltpu.roll(wrow, 0, axis=1, stride=1, stride_axis=0)
            tt = rolled[:, blk:].astype(BF16)
            j0, j1 = max(0, -delta), min(n_blk, n_blk - delta)
            res = jnp.dot(zr[j0 * nb:j1 * nb, :], tt, preferred_element_type=F32)
            for j in range(j0, j1):
                part = res[(j - j0) * nb:(j - j0 + 1) * nb, :]
                i = j + delta
                acc[i] = part if acc[i] is None else acc[i] + part
        y = jnp.concatenate(acc, axis=1)
        o_ref[c] = (y + zc * skip_ref[pl.ds(c, 1), :]) * x0_ref[c]
        return carry

    lax.fori_loop(0, z_ref.shape[0], channel, 0)


def _long_conv(z_t, x0_t, kk, skip):
    d, b, l = z_t.shape
    ch = CONV_CH
    kern = functools.partial(_long_conv_kernel, n_blk=l // CONV_BLOCK)
    return pl.pallas_call(
        kern,
        grid=(d // ch,),
        in_specs=[
            pl.BlockSpec((ch, b, l), lambda i: (i, 0, 0)),
            pl.BlockSpec((ch, b, l), lambda i: (i, 0, 0)),
            pl.BlockSpec((ch, 2 * l), lambda i: (i, 0)),
            pl.BlockSpec((ch, 1), lambda i: (i, 0)),
        ],
        out_specs=pl.BlockSpec((ch, b, l), lambda i: (i, 0, 0)),
        out_shape=jax.ShapeDtypeStruct((d, b, l), F32),
        compiler_params=_params(("parallel",)),
        name="hyena_long_conv",
    )(z_t, x0_t, kk, skip)


def _rope_tables(seq_len, head_dim):
    rows = seq_len // GRID_W
    row = jnp.repeat(jnp.arange(rows), GRID_W)
    col = jnp.tile(jnp.arange(GRID_W), rows)
    axis_dim = head_dim // 2
    inv = ROPE_THETA ** (-jnp.arange(0, axis_dim, 2, dtype=F32) / axis_dim)
    ang = jnp.concatenate([row[:, None] * inv, col[:, None] * inv], -1)
    cos, sin = jnp.cos(ang), jnp.sin(ang)
    reps = LANES // head_dim
    cc = jnp.tile(jnp.concatenate([cos, cos], -1), (1, reps))
    ss = jnp.tile(jnp.concatenate([-sin, sin], -1), (1, reps))
    return cc, ss


def _moe(x1, aff, group_tokens, w_gate, w_up, w_down, layer, g, beta, alpha, split_rows=None):
    n, d = x1.shape
    ne = aff.shape[1]
    tt = TOK_TILE
    slots, gates, bases = [], [], []
    tok0, slot0 = 0, 0
    for ng in group_tokens:
        cap = EC_CAPACITY * ng // ne
        s, gt, bs = _select(aff[tok0:tok0 + ng].T.reshape(ne, ng // tt, tt), cap, slot0)
        slots.append(s)
        gates.append(gt)
        bases.append(bs[:, :, 0])
        tok0 += ng
        slot0 += cap
    cap_total = slot0
    slots_t = jnp.concatenate(slots, axis=1).transpose(1, 0, 2)
    gates_t = jnp.concatenate(gates, axis=1).transpose(1, 0, 2)
    base = jnp.concatenate(bases + [jnp.full((ne, 1), cap_total, I32)], axis=1).T
    base_flat = base.reshape(-1)
    xe = _dispatch(x1, slots_t, base_flat, cap_total)
    ye = _expert_ffn(xe, w_gate, w_up, w_down, cap_total, layer)
    return _combine_ln(ye, slots_t, gates_t, base_flat, x1, g, beta, alpha, split_rows)


def kernel(x_prompt, x_sample, attn_w_qkv, attn_q_gain, attn_k_gain, attn_w_o, hy_w_in, hy_b_in, hy_conv_w, hy_conv_b, hy_f_w1, hy_f_b1, hy_f_freq, hy_f_w2, hy_f_b2, hy_f_w3, hy_skip, hy_w_out, hy_b_out, ln_mix_g, ln_mix_b, moe_router, moe_w_gate, moe_w_up, moe_w_down, ln_ffn_g, ln_ffn_b):
    bp, l, d = x_prompt.shape
    bs = x_sample.shape[0]
    assert x_sample.shape[1] == l
    depth = ln_mix_g.shape[0]
    alpha = (2 * depth) ** 0.25
    head_dim = d // N_HEADS
    b = bp + bs
    n = b * l
    group_tokens = (bp * l, bs * l)

    x = jnp.concatenate([x_prompt, x_sample], axis=0).reshape(n, d)
    cc, ss = _rope_tables(l, head_dim)
    seg = (lax.broadcasted_iota(I32, (LANES, LANES), 0) // head_dim
           == lax.broadcasted_iota(I32, (LANES, LANES), 1) // head_dim).astype(BF16)
    row = lambda v: v.astype(F32).reshape(1, -1)
    zero_bias = jnp.zeros((1, d), F32)
    stack = lambda w: w.astype(BF16).reshape((-1,) + w.shape[2:])
    w_gate, w_up, w_down = stack(moe_w_gate), stack(moe_w_up), stack(moe_w_down)

    for i in range(depth):
        j = i // 2
        r_hi, r_lo = _split_bf16(moe_router[i].astype(F32))
        if i % 2 == 0:
            gq = jnp.tile(row(attn_q_gain[j]), (1, LANES // head_dim))
            gk = jnp.tile(row(attn_k_gain[j]), (1, LANES // head_dim))
            q, k, v = _qkv_rope(x, attn_w_qkv[j].astype(BF16), cc, ss, gq, gk, seg, l)
            o = _attention(q.reshape(b, l, -1), k.reshape(b, l, -1), v.reshape(b, l, -1), tq=min(256, l))
            mix_in, w_mix, b_mix = o.reshape(n, -1), attn_w_o[j].astype(BF16), zero_bias
        else:
            kk = _hyena_filter(hy_f_w1[j], hy_f_b1[j], hy_f_freq[j], hy_f_w2[j], hy_f_b2[j], hy_f_w3[j], l, d)
            x0, z = _hyena_in(x.reshape(b, l, d), hy_w_in[j].astype(BF16), hy_b_in[j], hy_conv_w[j], hy_conv_b[j])
            y_t = _long_conv(z.transpose(2, 0, 1), x0.transpose(2, 0, 1), kk,
                             hy_skip[j].astype(F32).reshape(d, 1))
            mix_in = y_t.transpose(1, 2, 0).astype(BF16).reshape(n, d)
            w_mix, b_mix = hy_w_out[j].astype(BF16), row(hy_b_out[j])
        x1, aff = _proj_ln_router(mix_in, w_mix, b_mix, x, row(ln_mix_g[i]), row(ln_mix_b[i]),
                                  r_hi, r_lo, alpha)
        last = i == depth - 1
        out = _moe(x1, aff, group_tokens, w_gate, w_up, w_down, i, row(ln_ffn_g[i]), row(ln_ffn_b[i]), alpha,
                   split_rows=group_tokens[0] if last else None)
        if last:
            return (out[0].reshape(bp, l, d), out[1].reshape(bs, l, d))
        x = out[0]
```

```python
import functools
import math

import jax
import jax.numpy as jnp
from jax import lax
from jax.experimental import pallas as pl
from jax.experimental.pallas import tpu as pltpu

F32 = jnp.float32
BF16 = jnp.bfloat16
I32 = jnp.int32

N_HEADS = 16
N_KV_HEADS = 4
GRID_W = 64
ROPE_THETA = 10000.0
QK_EPS = 1e-6
FILTER_EMB = 33
DECAY_TARGET = 1e-2
FAST_DECAY_PCT = 0.3
SLOW_DECAY_PCT = 1.5
EC_CAPACITY = 2
LN_EPS = 1e-5

LANES = 128
MXU_DIM = 256
VMEM_LIMIT_BYTES = 56 * 1024 * 1024

ROW_TILE = 512
PROJ_ROWS = 256
TOK_TILE = 256
SLOT_ALIGN = 16
SLOT_WIN = 64
FFN_ROWS = 1024
CONV_BLOCK = 256
CONV_CH = 8
HY_CH = 256


def _params(sem, vmem=VMEM_LIMIT_BYTES):
    return pltpu.CompilerParams(dimension_semantics=sem, vmem_limit_bytes=vmem)


def _layer_norm(v, g, b):
    mu = jnp.mean(v, axis=-1, keepdims=True)
    c = v - mu
    var = jnp.mean(c * c, axis=-1, keepdims=True)
    return c * lax.rsqrt(var + LN_EPS) * g + b


def _split_bf16(v):
    hi = v.astype(BF16)
    lo = (v - hi.astype(F32)).astype(BF16)
    return hi, lo


def _qkv_kernel(x_ref, w_ref, cc_ref, ss_ref, gq_ref, gk_ref, seg_ref, q_ref, k_ref, v_ref, *,
                n_q_chunks, n_k_chunks, head_dim, scale):
    xb = x_ref[...].astype(BF16)
    qkv = jnp.dot(xb, w_ref[...], preferred_element_type=F32)
    cc = cc_ref[...]
    ss = ss_ref[...]
    seg = seg_ref[...]
    lane = lax.broadcasted_iota(I32, (xb.shape[0], LANES), 1)
    first_half = (lane % head_dim) < (head_dim // 2)
    half = head_dim // 2
    for c in range(n_q_chunks + n_k_chunks):
        u = qkv[:, c * LANES:(c + 1) * LANES]
        s_hi, s_lo = _split_bf16(u * u)
        ssum = (jnp.dot(s_hi, seg, preferred_element_type=F32)
                + jnp.dot(s_lo, seg, preferred_element_type=F32))
        r = lax.rsqrt(ssum * (1.0 / head_dim) + QK_EPS)
        gain = gq_ref[...] if c < n_q_chunks else gk_ref[...]
        un = u * r * gain
        partner = jnp.where(first_half, pltpu.roll(un, LANES - half, axis=1), pltpu.roll(un, half, axis=1))
        o = un * cc + partner * ss
        if c < n_q_chunks:
            q_ref[:, c * LANES:(c + 1) * LANES] = (o * scale).astype(BF16)
        else:
            ck = c - n_q_chunks
            k_ref[:, ck * LANES:(ck + 1) * LANES] = o.astype(BF16)
    v_ref[...] = qkv[:, (n_q_chunks + n_k_chunks) * LANES:].astype(BF16)


def _qkv_rope(x2d, w_qkv, cc, ss, gq, gk, seg, seq_len):
    n, d = x2d.shape
    head_dim = d // N_HEADS
    dq = N_HEADS * head_dim
    dk = N_KV_HEADS * head_dim
    tm = ROW_TILE
    tiles_per_seq = seq_len // tm
    kern = functools.partial(_qkv_kernel, n_q_chunks=dq // LANES, n_k_chunks=dk // LANES,
                             head_dim=head_dim, scale=head_dim ** -0.5 * math.log2(math.e))
    return pl.pallas_call(
        kern,
        grid=(n // tm,),
        in_specs=[
            pl.BlockSpec((tm, d), lambda i: (i, 0)),
            pl.BlockSpec((d, dq + 2 * dk), lambda i: (0, 0)),
            pl.BlockSpec((tm, LANES), lambda i: (i % tiles_per_seq, 0)),
            pl.BlockSpec((tm, LANES), lambda i: (i % tiles_per_seq, 0)),
            pl.BlockSpec((1, LANES), lambda i: (0, 0)),
            pl.BlockSpec((1, LANES), lambda i: (0, 0)),
            pl.BlockSpec((LANES, LANES), lambda i: (0, 0)),
        ],
        out_specs=[
            pl.BlockSpec((tm, dq), lambda i: (i, 0)),
            pl.BlockSpec((tm, dk), lambda i: (i, 0)),
            pl.BlockSpec((tm, dk), lambda i: (i, 0)),
        ],
        out_shape=[
            jax.ShapeDtypeStruct((n, dq), BF16),
            jax.ShapeDtypeStruct((n, dk), BF16),
            jax.ShapeDtypeStruct((n, dk), BF16),
        ],
        compiler_params=_params(("parallel",)),
        name="attn_qkv_rope",
    )(x2d, w_qkv, cc, ss, gq, gk, seg)


def _attn_kernel(q_ref, k_ref, v_ref, o_ref, *, head_dim, group):
    tq = q_ref.shape[1]
    n_keys = k_ref.shape[1]
    lane = lax.broadcasted_iota(I32, (n_keys, head_dim), 1)
    ones_col = jnp.where(lane == 0, 1.0, 0.0).astype(BF16)
    for kv in range(N_KV_HEADS):
        k = k_ref[0, :, kv * head_dim:(kv + 1) * head_dim]
        v = v_ref[0, :, kv * head_dim:(kv + 1) * head_dim]
        v1 = jnp.concatenate([v, ones_col], axis=1)
        qs = [q_ref[0, :, (kv * group + g) * head_dim:(kv * group + g + 1) * head_dim] for g in range(group)]
        q = jnp.concatenate(qs, axis=0)
        s = lax.dot_general(q, k, (((1,), (1,)), ((), ())), preferred_element_type=F32)
        m = jnp.max(s, axis=-1, keepdims=True)
        p = jnp.exp2(s - m).astype(BF16)
        ov = jnp.dot(p, v1, preferred_element_type=F32)
        o = ov[:, :head_dim] / ov[:, head_dim:head_dim + 1]
        outs = [o[g * tq:(g + 1) * tq, :] for g in range(group)]
        o_ref[0, :, kv * group * head_dim:(kv + 1) * group * head_dim] = (
            jnp.concatenate(outs, axis=1).astype(BF16))


def _attention(q, k, v, tq):
    b, l, dq = q.shape
    dk = k.shape[2]
    head_dim = dq // N_HEADS
    kern = functools.partial(_attn_kernel, head_dim=head_dim, group=N_HEADS // N_KV_HEADS)
    return pl.pallas_call(
        kern,
        grid=(b, l // tq),
        in_specs=[
            pl.BlockSpec((1, tq, dq), lambda i, j: (i, j, 0)),
            pl.BlockSpec((1, l, dk), lambda i, j: (i, 0, 0)),
            pl.BlockSpec((1, l, dk), lambda i, j: (i, 0, 0)),
        ],
        out_specs=pl.BlockSpec((1, tq, dq), lambda i, j: (i, j, 0)),
        out_shape=jax.ShapeDtypeStruct((b, l, dq), BF16),
        compiler_params=_params(("parallel", "parallel")),
        name="attn_core",
    )(q, k, v)


def _proj_ln_router_kernel(m_ref, w_ref, b_ref, x_ref, g_ref, beta_ref, rc_ref, rh_ref,
                           y_ref, aff_ref, *, alpha, channel_major):
    m = m_ref[0].T.astype(BF16) if channel_major else m_ref[...]
    mix = jnp.dot(m, w_ref[...], preferred_element_type=F32) + b_ref[...]
    y = _layer_norm(alpha * x_ref[...] + mix, g_ref[...], beta_ref[...])
    y_ref[...] = y
    y_hi, y_lo = _split_bf16(y)
    ne = rh_ref.shape[1]
    both = jnp.dot(y_hi, rc_ref[...], preferred_element_type=F32)
    lg = both[:, :ne] + both[:, ne:] + jnp.dot(y_lo, rh_ref[...], preferred_element_type=F32)
    ex = jnp.exp(lg - jnp.max(lg, axis=-1, keepdims=True))
    aff_ref[...] = ex / jnp.sum(ex, axis=-1, keepdims=True)


def _proj_ln_router(mix_in, w, bias, x2d, g, beta, r_hi, r_lo, alpha):
    n, d = x2d.shape
    ne = r_hi.shape[1]
    tm = PROJ_ROWS
    channel_major = mix_in.ndim == 3
    if channel_major:
        dm = mix_in.shape[1]
        tiles_per_seq = mix_in.shape[2] // tm
        mix_spec = pl.BlockSpec((1, dm, tm), lambda i: (i // tiles_per_seq, 0, i % tiles_per_seq))
    else:
        dm = mix_in.shape[1]
        mix_spec = pl.BlockSpec((tm, dm), lambda i: (i, 0))
    kern = functools.partial(_proj_ln_router_kernel, alpha=alpha, channel_major=channel_major)
    return pl.pallas_call(
        kern,
        grid=(n // tm,),
        in_specs=[
            mix_spec,
            pl.BlockSpec((dm, d), lambda i: (0, 0)),
            pl.BlockSpec((1, d), lambda i: (0, 0)),
            pl.BlockSpec((tm, d), lambda i: (i, 0)),
            pl.BlockSpec((1, d), lambda i: (0, 0)),
            pl.BlockSpec((1, d), lambda i: (0, 0)),
            pl.BlockSpec((d, 2 * ne), lambda i: (0, 0)),
            pl.BlockSpec((d, ne), lambda i: (0, 0)),
        ],
        out_specs=[
            pl.BlockSpec((tm, d), lambda i: (i, 0)),
            pl.BlockSpec((tm, ne), lambda i: (i, 0)),
        ],
        out_shape=[
            jax.ShapeDtypeStruct((n, d), F32),
            jax.ShapeDtypeStruct((n, ne), F32),
        ],
        compiler_params=_params(("parallel",)),
        name="proj_ln_router",
    )(mix_in, w, bias, x2d, g, beta, jnp.concatenate([r_hi, r_lo], axis=1), r_hi)


def _select_kernel(aff_ref, tri_ref, low_ref, slot_ref, gate_ref, base_ref, *, cap, slot_offset):
    aff = aff_ref[0]
    nt, tt = aff.shape
    bits = pltpu.bitcast(aff, I32)

    def count(mask):
        c = jnp.sum(mask.astype(F32), axis=1, keepdims=True)
        return jnp.sum(c, axis=0, keepdims=True)

    def step(i, prefix):
        cand = prefix | jnp.left_shift(jnp.int32(1), 30 - i)
        return jnp.where(count(bits >= cand) >= cap, cand, prefix)

    thr = lax.fori_loop(0, 31, step, jnp.zeros((1, 1), I32))
    gt = bits > thr
    eq = bits == thr
    need = cap - count(gt)

    tri = tri_ref[...]
    low = low_ref[...]
    ones = jnp.ones((tt, LANES), BF16)

    def excl_cumsum(mask):
        mf = mask.astype(F32).astype(BF16)
        within = jnp.dot(mf, tri, preferred_element_type=F32)
        tot = jnp.dot(mf, ones, preferred_element_type=F32).astype(BF16)
        base = jnp.dot(low, tot, preferred_element_type=F32)
        return within + base[:, :1], base

    rank_eq, _ = excl_cumsum(eq)
    sel = gt | (eq & (rank_eq < need))
    pos, base = excl_cumsum(sel)
    slot_ref[0] = jnp.where(sel, pos.astype(I32) + slot_offset, -1)
    gate_ref[0] = jnp.where(sel, aff, 0.0)
    base_ref[0] = base.astype(I32) + slot_offset


def _select(aff_t, cap, slot_offset):
    ne, nt, tt = aff_t.shape
    tri = (lax.broadcasted_iota(I32, (tt, tt), 0) < lax.broadcasted_iota(I32, (tt, tt), 1)).astype(BF16)
    low = (lax.broadcasted_iota(I32, (nt, nt), 0) > lax.broadcasted_iota(I32, (nt, nt), 1)).astype(BF16)
    kern = functools.partial(_select_kernel, cap=cap, slot_offset=slot_offset)
    return pl.pallas_call(
        kern,
        grid=(ne,),
        in_specs=[
            pl.BlockSpec((1, nt, tt), lambda e: (e, 0, 0)),
            pl.BlockSpec((tt, tt), lambda e: (0, 0)),
            pl.BlockSpec((nt, nt), lambda e: (0, 0)),
        ],
        out_specs=[
            pl.BlockSpec((1, nt, tt), lambda e: (e, 0, 0)),
            pl.BlockSpec((1, nt, tt), lambda e: (e, 0, 0)),
            pl.BlockSpec((1, nt, LANES), lambda e: (e, 0, 0)),
        ],
        out_shape=[
            jax.ShapeDtypeStruct((ne, nt, tt), I32),
            jax.ShapeDtypeStruct((ne, nt, tt), F32),
            jax.ShapeDtypeStruct((ne, nt, LANES), I32),
        ],
        compiler_params=_params(("parallel",)),
        name="moe_select",
    )(aff_t, tri, low)


def _onehot_rows(slot_row, first_slot, n_rows):
    tt = slot_row.shape[1]
    rows = lax.broadcasted_iota(I32, (n_rows, tt), 0) + first_slot
    return rows == slot_row


def _window_rounds(base_ref, i, n_experts, strict):
    n_rounds = jnp.int32(1)
    for e in range(n_experts):
        b0 = base_ref[i * n_experts + e]
        b1 = base_ref[(i + 1) * n_experts + e]
        span = b1 - (b0 // SLOT_ALIGN) * SLOT_ALIGN
        r = span // SLOT_WIN + 1 if strict else (span + SLOT_WIN - 1) // SLOT_WIN
        n_rounds = jnp.maximum(n_rounds, r)
    return n_rounds


def _dispatch_kernel(base_ref, x_ref, slot_ref, xe_ref, carry_ref, stage_ref, sem_ref, last_ref, *,
                     n_tiles, n_experts, cap_total, pad_windows):
    i = pl.program_id(0)

    def window_copy(e, start):
        rows = pl.ds(pl.multiple_of(start, SLOT_ALIGN), SLOT_WIN)
        return pltpu.make_async_copy(stage_ref.at[e], xe_ref.at[e, rows, :], sem_ref.at[e])

    @pl.when(i == 0)
    def _():
        carry_ref[...] = jnp.zeros_like(carry_ref)
        stage_ref[...] = jnp.zeros_like(stage_ref)
        pads = [window_copy(e, cap_total + k * SLOT_WIN) for e in range(n_experts) for k in range(pad_windows)]
        for cp in pads:
            cp.start()
        for cp in pads:
            cp.wait()

    xb = x_ref[...].astype(BF16)
    slots = slot_ref[0]

    def do_round(r, carry):
        starts = [(base_ref[i * n_experts + e] // SLOT_ALIGN) * SLOT_ALIGN + r * SLOT_WIN
                  for e in range(n_experts)]
        onehot = jnp.concatenate(
            [_onehot_rows(slots[e:e + 1, :], starts[e], SLOT_WIN) for e in range(n_experts)], axis=0)
        contrib = jnp.dot(onehot.astype(F32).astype(BF16), xb, preferred_element_type=F32)
        for e in range(n_experts):
            data = contrib[e * SLOT_WIN:(e + 1) * SLOT_WIN, :].astype(BF16)
            head = jnp.where(r == 0, carry_ref[e], jnp.zeros_like(carry_ref[e]))

            @pl.when(jnp.logical_or(i > 0, r > 0))
            def _(e=e):
                window_copy(e, last_ref[e]).wait()

            stage_ref[e, :SLOT_ALIGN, :] = data[:SLOT_ALIGN, :] + head
            stage_ref[e, SLOT_ALIGN:, :] = data[SLOT_ALIGN:, :]
            window_copy(e, starts[e]).start()
            last_ref[e] = starts[e]
            rem = base_ref[(i + 1) * n_experts + e] - starts[e]
            k = jnp.clip(rem // SLOT_ALIGN, 0, SLOT_WIN // SLOT_ALIGN - 1)
            tail = stage_ref[e, pl.ds(pl.multiple_of(k * SLOT_ALIGN, SLOT_ALIGN), SLOT_ALIGN), :]
            is_last = jnp.logical_and(rem >= 0, rem < SLOT_WIN)
            carry_ref[e] = jnp.where(is_last, tail, carry_ref[e])
        return carry

    lax.fori_loop(0, _window_rounds(base_ref, i, n_experts, strict=True), do_round, 0)

    @pl.when(i == n_tiles - 1)
    def _():
        for e in range(n_experts):
            window_copy(e, last_ref[e]).wait()


def _dispatch(x2d, slots_t, base_flat, cap_total):
    n, d = x2d.shape
    n_tiles, ne, tt = slots_t.shape
    pad_windows = tt // SLOT_WIN + 1
    pad_rows = pad_windows * SLOT_WIN
    kern = functools.partial(_dispatch_kernel, n_tiles=n_tiles, n_experts=ne, cap_total=cap_total,
                             pad_windows=pad_windows)
    return pl.pallas_call(
        kern,
        grid_spec=pltpu.PrefetchScalarGridSpec(
            num_scalar_prefetch=1,
            grid=(n_tiles,),
            in_specs=[
                pl.BlockSpec((tt, d), lambda i, b: (i, 0)),
                pl.BlockSpec((1, ne, tt), lambda i, b: (i, 0, 0)),
            ],
            out_specs=pl.BlockSpec(memory_space=pl.ANY),
            scratch_shapes=[
                pltpu.VMEM((ne, SLOT_ALIGN, d), BF16),
                pltpu.VMEM((ne, SLOT_WIN, d), BF16),
                pltpu.SemaphoreType.DMA((ne,)),
                pltpu.SMEM((ne,), I32),
            ],
        ),
        out_shape=jax.ShapeDtypeStruct((ne, cap_total + pad_rows, d), BF16),
        compiler_params=_params(("arbitrary",)),
        name="moe_dispatch",
    )(base_flat, x2d, slots_t)


def _ffn_kernel(x_ref, wg_ref, wu_ref, wd_ref, o_ref, *, chunk):
    x = x_ref[0]
    ff = wg_ref.shape[2]
    acc = None
    for c in range(ff // chunk):
        cols = slice(c * chunk, (c + 1) * chunk)
        hg = jnp.dot(x, wg_ref[0, :, cols], preferred_element_type=F32)
        hu = jnp.dot(x, wu_ref[0, :, cols], preferred_element_type=F32)
        h = (hg * jax.nn.sigmoid(hg) * hu).astype(BF16)
        part = jnp.dot(h, wd_ref[0, cols, :], preferred_element_type=F32)
        acc = part if acc is None else acc + part
    o_ref[0] = acc.astype(BF16)


def _expert_ffn(xe, wg, wu, wd, cap_total, layer):
    ne, _, d = xe.shape
    ff = wg.shape[2]
    tm = FFN_ROWS
    while cap_total % tm:
        tm //= 2
    chunk = MXU_DIM if ff % MXU_DIM == 0 else LANES
    w0 = layer * ne
    return pl.pallas_call(
        functools.partial(_ffn_kernel, chunk=chunk),
        grid=(ne, cap_total // tm),
        in_specs=[
            pl.BlockSpec((1, tm, d), lambda e, i: (e, i, 0)),
            pl.BlockSpec((1, d, ff), lambda e, i: (w0 + e, 0, 0)),
            pl.BlockSpec((1, d, ff), lambda e, i: (w0 + e, 0, 0)),
            pl.BlockSpec((1, ff, d), lambda e, i: (w0 + e, 0, 0)),
        ],
        out_specs=pl.BlockSpec((1, tm, d), lambda e, i: (e, i, 0)),
        out_shape=jax.ShapeDtypeStruct((ne, cap_total, d), BF16),
        compiler_params=_params(("parallel", "arbitrary")),
        name="moe_expert_ffn",
    )(xe, wg, wu, wd)


def _combine_kernel(base_ref, *refs, n_experts, cap_total, alpha, split_tile):
    win_refs = refs[:n_experts]
    slot_ref, gate_ref, x_ref, g_ref, beta_ref, ye_ref = refs[n_experts:n_experts + 6]
    out_refs = refs[n_experts + 6:-2]
    extra_ref, sem_ref = refs[-2:]
    i = pl.program_id(0)
    slots = slot_ref[0]
    gates = gate_ref[0]

    def bounds(e, r):
        lower = (base_ref[i * n_experts + e] // SLOT_ALIGN) * SLOT_ALIGN + r * SLOT_WIN
        return lower, jnp.minimum(lower, cap_total - SLOT_WIN)

    def weights(r):
        pieces = []
        for e in range(n_experts):
            lower, start = bounds(e, r)
            srow = slots[e:e + 1, :]
            hit = jnp.logical_and(_onehot_rows(srow, start, SLOT_WIN), srow >= lower)
            pieces.append(jnp.where(hit, gates[e:e + 1, :], 0.0))
        return jnp.concatenate(pieces, axis=0).T.astype(BF16)

    ye0 = jnp.concatenate([r[0] for r in win_refs], axis=0)
    y = jnp.dot(weights(0), ye0, preferred_element_type=F32)

    def extra_round(r, y):
        for e in range(n_experts):
            _, start = bounds(e, r)
            cp = pltpu.make_async_copy(ye_ref.at[e, pl.ds(pl.multiple_of(start, SLOT_ALIGN), SLOT_WIN), :],
                                       extra_ref.at[pl.ds(e * SLOT_WIN, SLOT_WIN), :], sem_ref.at[0])
            cp.start()
            cp.wait()
        return y + jnp.dot(weights(r), extra_ref[...], preferred_element_type=F32)

    y = lax.fori_loop(1, _window_rounds(base_ref, i, n_experts, strict=False), extra_round, y)
    out = _layer_norm(alpha * x_ref[...] + y, g_ref[...], beta_ref[...])
    if split_tile is None:
        out_refs[0][...] = out
    else:
        @pl.when(i < split_tile)
        def _():
            out_refs[0][...] = out

        @pl.when(i >= split_tile)
        def _():
            out_refs[1][...] = out


def _combine_ln(ye, slots_t, gates_t, base_flat, x2d, g, beta, alpha, split_rows=None):
    n, d = x2d.shape
    n_tiles, ne, tt = slots_t.shape
    cap_total = ye.shape[1]
    split_tile = None if split_rows is None else split_rows // tt
    kern = functools.partial(_combine_kernel, n_experts=ne, cap_total=cap_total, alpha=alpha,
                             split_tile=split_tile)

    def win_spec(e):
        def imap(i, b):
            lower = (b[i * ne + e] // SLOT_ALIGN) * SLOT_ALIGN
            return (e, pl.multiple_of(jnp.minimum(lower, cap_total - SLOT_WIN), SLOT_ALIGN), 0)
        return pl.BlockSpec((pl.Element(1), pl.Element(SLOT_WIN), pl.Element(d)), imap)

    in_specs = [win_spec(e) for e in range(ne)]
    in_specs += [
        pl.BlockSpec((1, ne, tt), lambda i, b: (i, 0, 0)),
        pl.BlockSpec((1, ne, tt), lambda i, b: (i, 0, 0)),
        pl.BlockSpec((tt, d), lambda i, b: (i, 0)),
        pl.BlockSpec((1, d), lambda i, b: (0, 0)),
        pl.BlockSpec((1, d), lambda i, b: (0, 0)),
        pl.BlockSpec(memory_space=pl.ANY),
    ]
    if split_tile is None:
        out_specs = [pl.BlockSpec((tt, d), lambda i, b: (i, 0))]
        out_shape = [jax.ShapeDtypeStruct((n, d), F32)]
    else:
        out_specs = [pl.BlockSpec((tt, d), lambda i, b: (jnp.minimum(i, split_tile - 1), 0)),
                     pl.BlockSpec((tt, d), lambda i, b: (jnp.maximum(i - split_tile, 0), 0))]
        out_shape = [jax.ShapeDtypeStruct((split_rows, d), F32),
                     jax.ShapeDtypeStruct((n - split_rows, d), F32)]
    return pl.pallas_call(
        kern,
        grid_spec=pltpu.PrefetchScalarGridSpec(
            num_scalar_prefetch=1,
            grid=(n_tiles,),
            in_specs=in_specs,
            out_specs=out_specs,
            scratch_shapes=[
                pltpu.VMEM((ne * SLOT_WIN, d), BF16),
                pltpu.SemaphoreType.DMA((1,)),
            ],
        ),
        out_shape=out_shape,
        compiler_params=_params(("arbitrary",)),
        name="moe_combine_ln",
    )(base_flat, *([ye] * ne), slots_t, gates_t, x2d, g, beta, ye)


def _filter_kernel(w1t_ref, w1c_ref, w1s_ref, b1_ref, fq_ref, w2_ref, b2_ref, w3_ref, fr_ref, dl_ref,
                   o_ref, *, seq_len):
    half = pl.program_id(0)
    pos = lax.broadcasted_iota(I32, (1, seq_len), 1)
    lag = jnp.where(half == 0, seq_len - pos, pos)
    lagf = lag.astype(F32)
    t = lagf / (seq_len - 1.0)
    w = (2.0 * math.pi) * lagf / seq_len
    fw = fr_ref[...] * w
    hp = lax.Precision.HIGHEST
    pre = (w1t_ref[...] * t
           + jnp.dot(w1c_ref[...], jnp.cos(fw), precision=hp, preferred_element_type=F32)
           - jnp.dot(w1s_ref[...], jnp.sin(fw), precision=hp, preferred_element_type=F32)
           + b1_ref[...])
    fq = fq_ref[...]
    h = jnp.sin(fq * pre)
    h = jnp.sin(fq * (jnp.dot(w2_ref[...], h, precision=hp, preferred_element_type=F32) + b2_ref[...]))
    out = jnp.dot(w3_ref[0], h, precision=hp, preferred_element_type=F32)
    out = out * jnp.exp(-t * dl_ref[...])
    o_ref[...] = jnp.where(lag < seq_len, out, 0.0)


def _hyena_filter(f_w1, f_b1, f_freq, f_w2, f_b2, f_w3, seq_len, d):
    hid = f_w1.shape[1]
    bands = (FILTER_EMB - 1) // 2
    w1 = f_w1.astype(F32).T
    col = lambda v: v.astype(F32).reshape(-1, 1)
    freqs = jnp.linspace(1e-4, bands - 1, bands, dtype=F32).reshape(bands, 1)
    min_decay = math.log(DECAY_TARGET) / SLOW_DECAY_PCT
    max_decay = math.log(DECAY_TARGET) / FAST_DECAY_PCT
    deltas = jnp.abs(jnp.linspace(min_decay, max_decay, d, dtype=F32)).reshape(d, 1)
    w3 = f_w3.astype(F32).T.reshape(2, d, hid)
    full = lambda shape: pl.BlockSpec(shape, lambda s: (0,) * len(shape))
    return pl.pallas_call(
        functools.partial(_filter_kernel, seq_len=seq_len),
        grid=(2,),
        in_specs=[
            full((hid, 1)), full((hid, bands)), full((hid, bands)), full((hid, 1)), full((hid, 1)),
            full((hid, hid)), full((hid, 1)),
            pl.BlockSpec((1, d, hid), lambda s: (1 - s, 0, 0)),
            full((bands, 1)), full((d, 1)),
        ],
        out_specs=pl.BlockSpec((d, seq_len), lambda s: (0, s)),
        out_shape=jax.ShapeDtypeStruct((d, 2 * seq_len), F32),
        compiler_params=_params(("arbitrary",)),
        name="hyena_filter",
    )(w1[:, :1], w1[:, 1:1 + bands], w1[:, 1 + bands:], col(f_b1), col(f_freq),
      f_w2.astype(F32).T, col(f_b2), w3, freqs, deltas)


def _hyena_in_kernel(x_ref, w0_ref, w1_ref, w2_ref, b_ref, cw_ref, cb_ref, x0_ref, z_ref, xb_ref):
    c = pl.program_id(1)

    @pl.when(c == 0)
    def _():
        xb_ref[...] = x_ref[0].astype(BF16)

    xb = xb_ref[...]
    seq_len = xb.shape[0]
    row = lax.broadcasted_iota(I32, (seq_len, 1), 0)
    not_first = row > 0
    not_last = row < seq_len - 1

    def branch(w_ref, k):
        u = jnp.dot(xb, w_ref[...], preferred_element_type=F32) + b_ref[k]
        prev = jnp.where(not_first, pltpu.roll(u, 1, axis=0), 0.0)
        nxt = jnp.where(not_last, pltpu.roll(u, seq_len - 1, axis=0), 0.0)
        cw = cw_ref[k]
        return prev * cw[0:1] + u * cw[1:2] + nxt * cw[2:3] + cb_ref[k]

    x0_ref[0] = branch(w0_ref, 0).T
    z_ref[0] = (branch(w2_ref, 2) * branch(w1_ref, 1)).T


def _hyena_in(x3d, w_in, b_in, conv_w, conv_b):
    b, l, d = x3d.shape
    ch = HY_CH
    nc = d // ch
    b3 = b_in.astype(F32).reshape(3, 1, d)
    cw3 = conv_w.astype(F32).reshape(conv_w.shape[0], 3, d).transpose(1, 0, 2)
    cb3 = conv_b.astype(F32).reshape(3, 1, d)
    wspec = lambda k: pl.BlockSpec((d, ch), lambda i, c, k=k: (0, k * nc + c))
    return pl.pallas_call(
        _hyena_in_kernel,
        grid=(b, nc),
        in_specs=[
            pl.BlockSpec((1, l, d), lambda i, c: (i, 0, 0)),
            wspec(0), wspec(1), wspec(2),
            pl.BlockSpec((3, 1, ch), lambda i, c: (0, 0, c)),
            pl.BlockSpec((3, 3, ch), lambda i, c: (0, 0, c)),
            pl.BlockSpec((3, 1, ch), lambda i, c: (0, 0, c)),
        ],
        out_specs=[
            pl.BlockSpec((1, ch, l), lambda i, c: (i, c, 0)),
            pl.BlockSpec((1, ch, l), lambda i, c: (i, c, 0)),
        ],
        out_shape=[
            jax.ShapeDtypeStruct((b, d, l), F32),
            jax.ShapeDtypeStruct((b, d, l), F32),
        ],
        scratch_shapes=[pltpu.VMEM((l, d), BF16)],
        compiler_params=_params(("parallel", "arbitrary")),
        name="hyena_in_conv",
    )(x3d, w_in, w_in, w_in, b3, cw3, cb3)


def _long_conv_kernel(z_ref, x0_ref, kk_ref, skip_ref, o_ref, *, n_blk):
    blk = CONV_BLOCK
    nb = z_ref.shape[0]

    for c in range(z_ref.shape[1]):
        zc = z_ref[:, c, :]
        zr = jnp.concatenate([zc[:, j * blk:(j + 1) * blk] for j in range(n_blk)], axis=0).astype(BF16)
        kk = kk_ref[c:c + 1, :]
        acc = [None] * n_blk
        for delta in range(-(n_blk - 1), n_blk):
            start = (n_blk + delta - 1) * blk
            wrow = jnp.broadcast_to(kk[:, start:start + 2 * blk], (blk, 2 * blk))
            rolled = pltpu.roll(wrow, 0, axis=1, stride=1, stride_axis=0)
            tt = rolled[:, blk:].astype(BF16)
            j0, j1 = max(0, -delta), min(n_blk, n_blk - delta)
            res = jnp.dot(zr[j0 * nb:j1 * nb, :], tt, preferred_element_type=F32)
            for j in range(j0, j1):
                part = res[(j - j0) * nb:(j - j0 + 1) * nb, :]
                i = j + delta
                acc[i] = part if acc[i] is None else acc[i] + part
        y = jnp.concatenate(acc, axis=1)
        o_ref[:, c, :] = (y + zc * skip_ref[c:c + 1, :]) * x0_ref[:, c, :]


def _long_conv(z_t, x0_t, kk, skip):
    b, d, l = z_t.shape
    ch = CONV_CH
    kern = functools.partial(_long_conv_kernel, n_blk=l // CONV_BLOCK)
    return pl.pallas_call(
        kern,
        grid=(d // ch,),
        in_specs=[
            pl.BlockSpec((b, ch, l), lambda i: (0, i, 0)),
            pl.BlockSpec((b, ch, l), lambda i: (0, i, 0)),
            pl.BlockSpec((ch, 2 * l), lambda i: (i, 0)),
            pl.BlockSpec((ch, 1), lambda i: (i, 0)),
        ],
        out_specs=pl.BlockSpec((b, ch, l), lambda i: (0, i, 0)),
        out_shape=jax.ShapeDtypeStruct((b, d, l), F32),
        compiler_params=_params(("parallel",)),
        name="hyena_long_conv",
    )(z_t, x0_t, kk, skip)


def _rope_tables(seq_len, head_dim):
    rows = seq_len // GRID_W
    row = jnp.repeat(jnp.arange(rows), GRID_W)
    col = jnp.tile(jnp.arange(GRID_W), rows)
    axis_dim = head_dim // 2
    inv = ROPE_THETA ** (-jnp.arange(0, axis_dim, 2, dtype=F32) / axis_dim)
    ang = jnp.concatenate([row[:, None] * inv, col[:, None] * inv], -1)
    cos, sin = jnp.cos(ang), jnp.sin(ang)
    reps = LANES // head_dim
    cc = jnp.tile(jnp.concatenate([cos, cos], -1), (1, reps))
    ss = jnp.tile(jnp.concatenate([-sin, sin], -1), (1, reps))
    return cc, ss


def _moe(x1, aff, group_tokens, w_gate, w_up, w_down, layer, g, beta, alpha, split_rows=None):
    n, d = x1.shape
    ne = aff.shape[1]
    tt = TOK_TILE
    slots, gates, bases = [], [], []
    tok0, slot0 = 0, 0
    for ng in group_tokens:
        cap = EC_CAPACITY * ng // ne
        s, gt, bs = _select(aff[tok0:tok0 + ng].T.reshape(ne, ng // tt, tt), cap, slot0)
        slots.append(s)
        gates.append(gt)
        bases.append(bs[:, :, 0])
        tok0 += ng
        slot0 += cap
    cap_total = slot0
    slots_t = jnp.concatenate(slots, axis=1).transpose(1, 0, 2)
    gates_t = jnp.concatenate(gates, axis=1).transpose(1, 0, 2)
    base = jnp.concatenate(bases + [jnp.full((ne, 1), cap_total, I32)], axis=1).T
    base_flat = base.reshape(-1)
    xe = _dispatch(x1, slots_t, base_flat, cap_total)
    ye = _expert_ffn(xe, w_gate, w_up, w_down, cap_total, layer)
    return _combine_ln(ye, slots_t, gates_t, base_flat, x1, g, beta, alpha, split_rows)


def kernel(x_prompt, x_sample, attn_w_qkv, attn_q_gain, attn_k_gain, attn_w_o, hy_w_in, hy_b_in, hy_conv_w, hy_conv_b, hy_f_w1, hy_f_b1, hy_f_freq, hy_f_w2, hy_f_b2, hy_f_w3, hy_skip, hy_w_out, hy_b_out, ln_mix_g, ln_mix_b, moe_router, moe_w_gate, moe_w_up, moe_w_down, ln_ffn_g, ln_ffn_b):
    bp, l, d = x_prompt.shape
    bs = x_sample.shape[0]
    assert x_sample.shape[1] == l
    depth = ln_mix_g.shape[0]
    alpha = (2 * depth) ** 0.25
    head_dim = d // N_HEADS
    b = bp + bs
    n = b * l
    group_tokens = (bp * l, bs * l)

    x = jnp.concatenate([x_prompt, x_sample], axis=0).reshape(n, d)
    cc, ss = _rope_tables(l, head_dim)
    seg = (lax.broadcasted_iota(I32, (LANES, LANES), 0) // head_dim
           == lax.broadcasted_iota(I32, (LANES, LANES), 1) // head_dim).astype(BF16)
    row = lambda v: v.astype(F32).reshape(1, -1)
    zero_bias = jnp.zeros((1, d), F32)
    stack = lambda w: w.astype(BF16).reshape((-1,) + w.shape[2:])
    w_gate, w_up, w_down = stack(moe_w_gate), stack(moe_w_up), stack(moe_w_down)

    for i in range(depth):
        j = i // 2
        r_hi, r_lo = _split_bf16(moe_router[i].astype(F32))
        if i % 2 == 0:
            gq = jnp.tile(row(attn_q_gain[j]), (1, LANES // head_dim))
            gk = jnp.tile(row(attn_k_gain[j]), (1, LANES // head_dim))
            q, k, v = _qkv_rope(x, attn_w_qkv[j].astype(BF16), cc, ss, gq, gk, seg, l)
            o = _attention(q.reshape(b, l, -1), k.reshape(b, l, -1), v.reshape(b, l, -1), tq=min(256, l))
            mix_in, w_mix, b_mix = o.reshape(n, -1), attn_w_o[j].astype(BF16), zero_bias
        else:
            kk = _hyena_filter(hy_f_w1[j], hy_f_b1[j], hy_f_freq[j], hy_f_w2[j], hy_f_b2[j], hy_f_w3[j], l, d)
            x0, z = _hyena_in(x.reshape(b, l, d), hy_w_in[j].astype(BF16), hy_b_in[j], hy_conv_w[j], hy_conv_b[j])
            mix_in = _long_conv(z, x0, kk, hy_skip[j].astype(F32).reshape(d, 1))
            w_mix, b_mix = hy_w_out[j].astype(BF16), row(hy_b_out[j])
        x1, aff = _proj_ln_router(mix_in, w_mix, b_mix, x, row(ln_mix_g[i]), row(ln_mix_b[i]),
                                  r_hi, r_lo, alpha)
        last = i == depth - 1
        out = _moe(x1, aff, group_tokens, w_gate, w_up, w_down, i, row(ln_ffn_g[i]), row(ln_ffn_b[i]), alpha,
                   split_rows=group_tokens[0] if last else None)
        if last:
            return (out[0].reshape(bp, l, d), out[1].reshape(bs, l, d))
        x = out[0]
```

```python
import functools
import math

import jax
import jax.numpy as jnp
from jax import lax
from jax.experimental import pallas as pl
from jax.experimental.pallas import tpu as pltpu

F32 = jnp.float32
BF16 = jnp.bfloat16
I32 = jnp.int32

N_HEADS = 16
N_KV_HEADS = 4
GRID_W = 64
ROPE_THETA = 10000.0
QK_EPS = 1e-6
FILTER_EMB = 33
DECAY_TARGET = 1e-2
FAST_DECAY_PCT = 0.3
SLOW_DECAY_PCT = 1.5
EC_CAPACITY = 2
LN_EPS = 1e-5

LANES = 128
MXU_DIM = 256
VMEM_LIMIT_BYTES = 56 * 1024 * 1024

ROW_TILE = 512
PROJ_ROWS = 256
ATTN_ROWS = 256
ATTN_STEP = 512
TOK_TILE = 256
SLOT_ALIGN = 16
SLOT_WIN = 64
FFN_ROWS = 1024
CONV_BLOCK = 256
CONV_CH = 8
HY_CH = 256


def _params(sem, vmem=VMEM_LIMIT_BYTES):
    return pltpu.CompilerParams(dimension_semantics=sem, vmem_limit_bytes=vmem)


def _layer_norm(v, g, b):
    mu = jnp.mean(v, axis=-1, keepdims=True)
    c = v - mu
    var = jnp.mean(c * c, axis=-1, keepdims=True)
    return c * lax.rsqrt(var + LN_EPS) * g + b


def _split_bf16(v):
    hi = v.astype(BF16)
    lo = (v - hi.astype(F32)).astype(BF16)
    return hi, lo


def _qkv_kernel(x_ref, w_ref, cc_ref, ss_ref, gq_ref, gk_ref, seg_ref, q_ref, k_ref, v_ref, *,
                n_q_chunks, n_k_chunks, head_dim, scale):
    xb = x_ref[...].astype(BF16)
    qkv = jnp.dot(xb, w_ref[...], preferred_element_type=F32)
    cc = cc_ref[...]
    ss = ss_ref[...]
    seg = seg_ref[...]
    lane = lax.broadcasted_iota(I32, (xb.shape[0], LANES), 1)
    first_half = (lane % head_dim) < (head_dim // 2)
    half = head_dim // 2
    for c in range(n_q_chunks + n_k_chunks):
        u = qkv[:, c * LANES:(c + 1) * LANES]
        s_hi, s_lo = _split_bf16(u * u)
        ssum = (jnp.dot(s_hi, seg, preferred_element_type=F32)
                + jnp.dot(s_lo, seg, preferred_element_type=F32))
        r = lax.rsqrt(ssum * (1.0 / head_dim) + QK_EPS)
        gain = gq_ref[...] if c < n_q_chunks else gk_ref[...]
        un = u * r * gain
        partner = jnp.where(first_half, pltpu.roll(un, LANES - half, axis=1), pltpu.roll(un, half, axis=1))
        o = un * cc + partner * ss
        if c < n_q_chunks:
            q_ref[:, c * LANES:(c + 1) * LANES] = (o * scale).astype(BF16)
        else:
            ck = c - n_q_chunks
            k_ref[:, ck * LANES:(ck + 1) * LANES] = o.astype(BF16)
    v_ref[...] = qkv[:, (n_q_chunks + n_k_chunks) * LANES:].astype(BF16)


def _qkv_rope(x2d, w_qkv, cc, ss, gq, gk, seg, seq_len):
    n, d = x2d.shape
    head_dim = d // N_HEADS
    dq = N_HEADS * head_dim
    dk = N_KV_HEADS * head_dim
    tm = ROW_TILE
    tiles_per_seq = seq_len // tm
    kern = functools.partial(_qkv_kernel, n_q_chunks=dq // LANES, n_k_chunks=dk // LANES,
                             head_dim=head_dim, scale=head_dim ** -0.5 * math.log2(math.e))
    return pl.pallas_call(
        kern,
        grid=(n // tm,),
        in_specs=[
            pl.BlockSpec((tm, d), lambda i: (i, 0)),
            pl.BlockSpec((d, dq + 2 * dk), lambda i: (0, 0)),
            pl.BlockSpec((tm, LANES), lambda i: (i % tiles_per_seq, 0)),
            pl.BlockSpec((tm, LANES), lambda i: (i % tiles_per_seq, 0)),
            pl.BlockSpec((1, LANES), lambda i: (0, 0)),
            pl.BlockSpec((1, LANES), lambda i: (0, 0)),
            pl.BlockSpec((LANES, LANES), lambda i: (0, 0)),
        ],
        out_specs=[
            pl.BlockSpec((tm, dq), lambda i: (i, 0)),
            pl.BlockSpec((tm, dk), lambda i: (i, 0)),
            pl.BlockSpec((tm, dk), lambda i: (i, 0)),
        ],
        out_shape=[
            jax.ShapeDtypeStruct((n, dq), BF16),
            jax.ShapeDtypeStruct((n, dk), BF16),
            jax.ShapeDtypeStruct((n, dk), BF16),
        ],
        compiler_params=_params(("parallel",)),
        name="attn_qkv_rope",
    )(x2d, w_qkv, cc, ss, gq, gk, seg)


def _attn_kernel(q_ref, k_ref, v_ref, o_ref, *, head_dim, group):
    tq = q_ref.shape[1]
    n_keys = k_ref.shape[1]
    lane = lax.broadcasted_iota(I32, (n_keys, head_dim), 1)
    ones_col = jnp.where(lane == 0, 1.0, 0.0).astype(BF16)
    for kv in range(N_KV_HEADS):
        k = k_ref[0, :, kv * head_dim:(kv + 1) * head_dim]
        v = v_ref[0, :, kv * head_dim:(kv + 1) * head_dim]
        v1 = jnp.concatenate([v, ones_col], axis=1)
        for r0 in range(0, tq, ATTN_ROWS):
            rows = slice(r0, r0 + ATTN_ROWS)
            qs = [q_ref[0, rows, (kv * group + g) * head_dim:(kv * group + g + 1) * head_dim]
                  for g in range(group)]
            q = jnp.concatenate(qs, axis=0)
            s = lax.dot_general(q, k, (((1,), (1,)), ((), ())), preferred_element_type=F32)
            m = jnp.max(s, axis=-1, keepdims=True)
            p = jnp.exp2(s - m).astype(BF16)
            ov = jnp.dot(p, v1, preferred_element_type=F32)
            o = ov[:, :head_dim] / ov[:, head_dim:head_dim + 1]
            outs = [o[g * ATTN_ROWS:(g + 1) * ATTN_ROWS, :] for g in range(group)]
            o_ref[0, rows, kv * group * head_dim:(kv + 1) * group * head_dim] = (
                jnp.concatenate(outs, axis=1).astype(BF16))


def _attention(q, k, v, tq):
    b, l, dq = q.shape
    dk = k.shape[2]
    head_dim = dq // N_HEADS
    kern = functools.partial(_attn_kernel, head_dim=head_dim, group=N_HEADS // N_KV_HEADS)
    return pl.pallas_call(
        kern,
        grid=(b, l // tq),
        in_specs=[
            pl.BlockSpec((1, tq, dq), lambda i, j: (i, j, 0)),
            pl.BlockSpec((1, l, dk), lambda i, j: (i, 0, 0)),
            pl.BlockSpec((1, l, dk), lambda i, j: (i, 0, 0)),
        ],
        out_specs=pl.BlockSpec((1, tq, dq), lambda i, j: (i, j, 0)),
        out_shape=jax.ShapeDtypeStruct((b, l, dq), BF16),
        compiler_params=_params(("parallel", "parallel")),
        name="attn_core",
    )(q, k, v)


def _proj_ln_router_kernel(m_ref, w_ref, b_ref, x_ref, g_ref, beta_ref, rc_ref, rh_ref,
                           y_ref, aff_ref, *, alpha, channel_major):
    m = m_ref[0].T.astype(BF16) if channel_major else m_ref[...]
    mix = jnp.dot(m, w_ref[...], preferred_element_type=F32) + b_ref[...]
    y = _layer_norm(alpha * x_ref[...] + mix, g_ref[...], beta_ref[...])
    y_ref[...] = y
    y_hi, y_lo = _split_bf16(y)
    ne = rh_ref.shape[1]
    both = jnp.dot(y_hi, rc_ref[...], preferred_element_type=F32)
    lg = both[:, :ne] + both[:, ne:] + jnp.dot(y_lo, rh_ref[...], preferred_element_type=F32)
    ex = jnp.exp(lg - jnp.max(lg, axis=-1, keepdims=True))
    aff_ref[...] = ex / jnp.sum(ex, axis=-1, keepdims=True)


def _proj_ln_router(mix_in, w, bias, x2d, g, beta, r_hi, r_lo, alpha):
    n, d = x2d.shape
    ne = r_hi.shape[1]
    tm = PROJ_ROWS
    channel_major = mix_in.ndim == 3
    if channel_major:
        dm = mix_in.shape[1]
        tiles_per_seq = mix_in.shape[2] // tm
        mix_spec = pl.BlockSpec((1, dm, tm), lambda i: (i // tiles_per_seq, 0, i % tiles_per_seq))
    else:
        dm = mix_in.shape[1]
        mix_spec = pl.BlockSpec((tm, dm), lambda i: (i, 0))
    kern = functools.partial(_proj_ln_router_kernel, alpha=alpha, channel_major=channel_major)
    return pl.pallas_call(
        kern,
        grid=(n // tm,),
        in_specs=[
            mix_spec,
            pl.BlockSpec((dm, d), lambda i: (0, 0)),
            pl.BlockSpec((1, d), lambda i: (0, 0)),
            pl.BlockSpec((tm, d), lambda i: (i, 0)),
            pl.BlockSpec((1, d), lambda i: (0, 0)),
            pl.BlockSpec((1, d), lambda i: (0, 0)),
            pl.BlockSpec((d, 2 * ne), lambda i: (0, 0)),
            pl.BlockSpec((d, ne), lambda i: (0, 0)),
        ],
        out_specs=[
            pl.BlockSpec((tm, d), lambda i: (i, 0)),
            pl.BlockSpec((tm, ne), lambda i: (i, 0)),
        ],
        out_shape=[
            jax.ShapeDtypeStruct((n, d), F32),
            jax.ShapeDtypeStruct((n, ne), F32),
        ],
        compiler_params=_params(("parallel",)),
        name="proj_ln_router",
    )(mix_in, w, bias, x2d, g, beta, jnp.concatenate([r_hi, r_lo], axis=1), r_hi)


def _select_kernel(aff_ref, tri_ref, low_ref, slot_ref, gate_ref, base_ref, *, cap, slot_offset):
    aff = aff_ref[0]
    nt, tt = aff.shape
    bits = pltpu.bitcast(aff, I32)

    def count(mask):
        c = jnp.sum(mask.astype(F32), axis=1, keepdims=True)
        return jnp.sum(c, axis=0, keepdims=True)

    def step(i, prefix):
        cand = prefix | jnp.left_shift(jnp.int32(1), 30 - i)
        return jnp.where(count(bits >= cand) >= cap, cand, prefix)

    thr = lax.fori_loop(0, 31, step, jnp.zeros((1, 1), I32))
    gt = bits > thr
    eq = bits == thr
    need = cap - count(gt)

    tri = tri_ref[...]
    low = low_ref[...]
    ones = jnp.ones((tt, LANES), BF16)

    def excl_cumsum(mask):
        mf = mask.astype(F32).astype(BF16)
        within = jnp.dot(mf, tri, preferred_element_type=F32)
        tot = jnp.dot(mf, ones, preferred_element_type=F32).astype(BF16)
        base = jnp.dot(low, tot, preferred_element_type=F32)
        return within + base[:, :1], base

    rank_eq, _ = excl_cumsum(eq)
    sel = gt | (eq & (rank_eq < need))
    pos, base = excl_cumsum(sel)
    slot_ref[0] = jnp.where(sel, pos.astype(I32) + slot_offset, -1)
    gate_ref[0] = jnp.where(sel, aff, 0.0)
    base_ref[0] = base.astype(I32) + slot_offset


def _select(aff_t, cap, slot_offset):
    ne, nt, tt = aff_t.shape
    tri = (lax.broadcasted_iota(I32, (tt, tt), 0) < lax.broadcasted_iota(I32, (tt, tt), 1)).astype(BF16)
    low = (lax.broadcasted_iota(I32, (nt, nt), 0) > lax.broadcasted_iota(I32, (nt, nt), 1)).astype(BF16)
    kern = functools.partial(_select_kernel, cap=cap, slot_offset=slot_offset)
    return pl.pallas_call(
        kern,
        grid=(ne,),
        in_specs=[
            pl.BlockSpec((1, nt, tt), lambda e: (e, 0, 0)),
            pl.BlockSpec((tt, tt), lambda e: (0, 0)),
            pl.BlockSpec((nt, nt), lambda e: (0, 0)),
        ],
        out_specs=[
            pl.BlockSpec((1, nt, tt), lambda e: (e, 0, 0)),
            pl.BlockSpec((1, nt, tt), lambda e: (e, 0, 0)),
            pl.BlockSpec((1, nt, LANES), lambda e: (e, 0, 0)),
        ],
        out_shape=[
            jax.ShapeDtypeStruct((ne, nt, tt), I32),
            jax.ShapeDtypeStruct((ne, nt, tt), F32),
            jax.ShapeDtypeStruct((ne, nt, LANES), I32),
        ],
        compiler_params=_params(("parallel",)),
        name="moe_select",
    )(aff_t, tri, low)


def _onehot_rows(slot_row, first_slot, n_rows):
    tt = slot_row.shape[1]
    rows = lax.broadcasted_iota(I32, (n_rows, tt), 0) + first_slot
    return rows == slot_row


def _routing_meta(base, cap_total):
    b0, b1 = base[:-1], base[1:]
    lower = (b0 // SLOT_ALIGN) * SLOT_ALIGN
    span = b1 - lower
    clamped = jnp.minimum(lower, cap_total - SLOT_WIN)
    rounds_dispatch = jnp.max(span // SLOT_WIN + 1, axis=1)
    rounds_combine = jnp.maximum(jnp.max((span + SLOT_WIN - 1) // SLOT_WIN, axis=1), 1)
    parts = [lower, b1, clamped, rounds_dispatch, rounds_combine]
    return jnp.concatenate([p.reshape(-1).astype(I32) for p in parts])


def _dispatch_kernel(meta_ref, x_ref, slot_ref, xe_ref, carry_ref, stage_ref, sem_ref, last_ref, *,
                     n_tiles, n_experts, cap_total, pad_windows):
    i = pl.program_id(0)

    def window_copy(e, start):
        rows = pl.ds(pl.multiple_of(start, SLOT_ALIGN), SLOT_WIN)
        return pltpu.make_async_copy(stage_ref.at[e], xe_ref.at[e, rows, :], sem_ref.at[e])

    @pl.when(i == 0)
    def _():
        carry_ref[...] = jnp.zeros_like(carry_ref)
        stage_ref[...] = jnp.zeros_like(stage_ref)
        pads = [window_copy(e, cap_total + k * SLOT_WIN) for e in range(n_experts) for k in range(pad_windows)]
        for cp in pads:
            cp.start()
        for cp in pads:
            cp.wait()

    xb = x_ref[...].astype(BF16)
    slots = slot_ref[0]

    def do_round(r, carry):
        starts = [meta_ref[i * n_experts + e] + r * SLOT_WIN for e in range(n_experts)]
        onehot = jnp.concatenate(
            [_onehot_rows(slots[e:e + 1, :], starts[e], SLOT_WIN) for e in range(n_experts)], axis=0)
        contrib = jnp.dot(onehot.astype(F32).astype(BF16), xb, preferred_element_type=F32)
        for e in range(n_experts):
            data = contrib[e * SLOT_WIN:(e + 1) * SLOT_WIN, :].astype(BF16)
            head = jnp.where(r == 0, carry_ref[e], jnp.zeros_like(carry_ref[e]))

            @pl.when(jnp.logical_or(i > 0, r > 0))
            def _(e=e):
                window_copy(e, last_ref[e]).wait()

            stage_ref[e, :SLOT_ALIGN, :] = data[:SLOT_ALIGN, :] + head
            stage_ref[e, SLOT_ALIGN:, :] = data[SLOT_ALIGN:, :]
            window_copy(e, starts[e]).start()
            last_ref[e] = starts[e]
            rem = meta_ref[(n_tiles + i) * n_experts + e] - starts[e]
            k = jnp.clip(rem // SLOT_ALIGN, 0, SLOT_WIN // SLOT_ALIGN - 1)
            tail = stage_ref[e, pl.ds(pl.multiple_of(k * SLOT_ALIGN, SLOT_ALIGN), SLOT_ALIGN), :]
            is_last = jnp.logical_and(rem >= 0, rem < SLOT_WIN)
            carry_ref[e] = jnp.where(is_last, tail, carry_ref[e])
        return carry

    lax.fori_loop(0, meta_ref[3 * n_tiles * n_experts + i], do_round, 0)

    @pl.when(i == n_tiles - 1)
    def _():
        for e in range(n_experts):
            window_copy(e, last_ref[e]).wait()


def _dispatch(x2d, slots_t, meta, cap_total):
    n, d = x2d.shape
    n_tiles, ne, tt = slots_t.shape
    pad_windows = tt // SLOT_WIN + 1
    pad_rows = pad_windows * SLOT_WIN
    kern = functools.partial(_dispatch_kernel, n_tiles=n_tiles, n_experts=ne, cap_total=cap_total,
                             pad_windows=pad_windows)
    return pl.pallas_call(
        kern,
        grid_spec=pltpu.PrefetchScalarGridSpec(
            num_scalar_prefetch=1,
            grid=(n_tiles,),
            in_specs=[
                pl.BlockSpec((tt, d), lambda i, b: (i, 0)),
                pl.BlockSpec((1, ne, tt), lambda i, b: (i, 0, 0)),
            ],
            out_specs=pl.BlockSpec(memory_space=pl.ANY),
            scratch_shapes=[
                pltpu.VMEM((ne, SLOT_ALIGN, d), BF16),
                pltpu.VMEM((ne, SLOT_WIN, d), BF16),
                pltpu.SemaphoreType.DMA((ne,)),
                pltpu.SMEM((ne,), I32),
            ],
        ),
        out_shape=jax.ShapeDtypeStruct((ne, cap_total + pad_rows, d), BF16),
        compiler_params=_params(("arbitrary",)),
        name="moe_dispatch",
    )(meta, x2d, slots_t)


def _ffn_kernel(x_ref, wg_ref, wu_ref, wd_ref, o_ref, *, chunk):
    x = x_ref[0]
    ff = wg_ref.shape[2]
    acc = None
    for c in range(ff // chunk):
        cols = slice(c * chunk, (c + 1) * chunk)
        hg = jnp.dot(x, wg_ref[0, :, cols], preferred_element_type=F32)
        hu = jnp.dot(x, wu_ref[0, :, cols], preferred_element_type=F32)
        h = (hg * jax.nn.sigmoid(hg) * hu).astype(BF16)
        part = jnp.dot(h, wd_ref[0, cols, :], preferred_element_type=F32)
        acc = part if acc is None else acc + part
    o_ref[0] = acc.astype(BF16)


def _expert_ffn(xe, wg, wu, wd, cap_total, layer):
    ne, _, d = xe.shape
    ff = wg.shape[2]
    tm = FFN_ROWS
    while cap_total % tm:
        tm //= 2
    chunk = MXU_DIM if ff % MXU_DIM == 0 else LANES
    w0 = layer * ne
    return pl.pallas_call(
        functools.partial(_ffn_kernel, chunk=chunk),
        grid=(ne, cap_total // tm),
        in_specs=[
            pl.BlockSpec((1, tm, d), lambda e, i: (e, i, 0)),
            pl.BlockSpec((1, d, ff), lambda e, i: (w0 + e, 0, 0)),
            pl.BlockSpec((1, d, ff), lambda e, i: (w0 + e, 0, 0)),
            pl.BlockSpec((1, ff, d), lambda e, i: (w0 + e, 0, 0)),
        ],
        out_specs=pl.BlockSpec((1, tm, d), lambda e, i: (e, i, 0)),
        out_shape=jax.ShapeDtypeStruct((ne, cap_total, d), BF16),
        compiler_params=_params(("parallel", "arbitrary")),
        name="moe_expert_ffn",
    )(xe, wg, wu, wd)


def _combine_kernel(meta_ref, *refs, n_tiles, n_experts, cap_total, alpha, split_tile):
    win_refs = refs[:n_experts]
    slot_ref, gate_ref, x_ref, g_ref, beta_ref, ye_ref = refs[n_experts:n_experts + 6]
    out_refs = refs[n_experts + 6:-2]
    extra_ref, sem_ref = refs[-2:]
    i = pl.program_id(0)
    slots = slot_ref[0]
    gates = gate_ref[0]

    te = n_tiles * n_experts

    def weights(lowers, starts):
        pieces = []
        for e in range(n_experts):
            srow = slots[e:e + 1, :]
            hit = jnp.logical_and(_onehot_rows(srow, starts[e], SLOT_WIN), srow >= lowers[e])
            pieces.append(jnp.where(hit, gates[e:e + 1, :], 0.0))
        return jnp.concatenate(pieces, axis=0).T.astype(BF16)

    lowers0 = [meta_ref[i * n_experts + e] for e in range(n_experts)]
    starts0 = [meta_ref[2 * te + i * n_experts + e] for e in range(n_experts)]
    ye0 = jnp.concatenate([r[0] for r in win_refs], axis=0)
    y = jnp.dot(weights(lowers0, starts0), ye0, preferred_element_type=F32)

    def extra_round(r, y):
        lowers = [lowers0[e] + r * SLOT_WIN for e in range(n_experts)]
        starts = [jnp.minimum(lowers[e], cap_total - SLOT_WIN) for e in range(n_experts)]
        for e in range(n_experts):
            rows = pl.ds(pl.multiple_of(starts[e], SLOT_ALIGN), SLOT_WIN)
            cp = pltpu.make_async_copy(ye_ref.at[e, rows, :],
                                       extra_ref.at[pl.ds(e * SLOT_WIN, SLOT_WIN), :], sem_ref.at[0])
            cp.start()
            cp.wait()
        return y + jnp.dot(weights(lowers, starts), extra_ref[...], preferred_element_type=F32)

    y = lax.fori_loop(1, meta_ref[3 * te + n_tiles + i], extra_round, y)
    out = _layer_norm(alpha * x_ref[...] + y, g_ref[...], beta_ref[...])
    if split_tile is None:
        out_refs[0][...] = out
    else:
        @pl.when(i < split_tile)
        def _():
            out_refs[0][...] = out

        @pl.when(i >= split_tile)
        def _():
            out_refs[1][...] = out


def _combine_ln(ye, slots_t, gates_t, meta, x2d, g, beta, alpha, split_rows=None):
    n, d = x2d.shape
    n_tiles, ne, tt = slots_t.shape
    cap_total = ye.shape[1]
    split_tile = None if split_rows is None else split_rows // tt
    kern = functools.partial(_combine_kernel, n_tiles=n_tiles, n_experts=ne, cap_total=cap_total, alpha=alpha,
                             split_tile=split_tile)

    def win_spec(e):
        def imap(i, meta):
            return (e, pl.multiple_of(meta[(2 * n_tiles + i) * ne + e], SLOT_ALIGN), 0)
        return pl.BlockSpec((pl.Element(1), pl.Element(SLOT_WIN), pl.Element(d)), imap)

    in_specs = [win_spec(e) for e in range(ne)]
    in_specs += [
        pl.BlockSpec((1, ne, tt), lambda i, b: (i, 0, 0)),
        pl.BlockSpec((1, ne, tt), lambda i, b: (i, 0, 0)),
        pl.BlockSpec((tt, d), lambda i, b: (i, 0)),
        pl.BlockSpec((1, d), lambda i, b: (0, 0)),
        pl.BlockSpec((1, d), lambda i, b: (0, 0)),
        pl.BlockSpec(memory_space=pl.ANY),
    ]
    if split_tile is None:
        out_specs = [pl.BlockSpec((tt, d), lambda i, b: (i, 0))]
        out_shape = [jax.ShapeDtypeStruct((n, d), F32)]
    else:
        out_specs = [pl.BlockSpec((tt, d), lambda i, b: (jnp.minimum(i, split_tile - 1), 0)),
                     pl.BlockSpec((tt, d), lambda i, b: (jnp.maximum(i - split_tile, 0), 0))]
        out_shape = [jax.ShapeDtypeStruct((split_rows, d), F32),
                     jax.ShapeDtypeStruct((n - split_rows, d), F32)]
    return pl.pallas_call(
        kern,
        grid_spec=pltpu.PrefetchScalarGridSpec(
            num_scalar_prefetch=1,
            grid=(n_tiles,),
            in_specs=in_specs,
            out_specs=out_specs,
            scratch_shapes=[
                pltpu.VMEM((ne * SLOT_WIN, d), BF16),
                pltpu.SemaphoreType.DMA((1,)),
            ],
        ),
        out_shape=out_shape,
        compiler_params=_params(("arbitrary",)),
        name="moe_combine_ln",
    )(meta, *([ye] * ne), slots_t, gates_t, x2d, g, beta, ye)


def _filter_kernel(w1t_ref, w1c_ref, w1s_ref, b1_ref, fq_ref, w2_ref, b2_ref, w3_ref, fr_ref, dl_ref,
                   o_ref, *, seq_len):
    half = pl.program_id(0)
    pos = lax.broadcasted_iota(I32, (1, seq_len), 1)
    lag = jnp.where(half == 0, seq_len - pos, pos)
    lagf = lag.astype(F32)
    t = lagf / (seq_len - 1.0)
    w = (2.0 * math.pi) * lagf / seq_len
    fw = fr_ref[...] * w
    hp = lax.Precision.HIGHEST
    pre = (w1t_ref[...] * t
           + jnp.dot(w1c_ref[...], jnp.cos(fw), precision=hp, preferred_element_type=F32)
           - jnp.dot(w1s_ref[...], jnp.sin(fw), precision=hp, preferred_element_type=F32)
           + b1_ref[...])
    fq = fq_ref[...]
    h = jnp.sin(fq * pre)
    h = jnp.sin(fq * (jnp.dot(w2_ref[...], h, precision=hp, preferred_element_type=F32) + b2_ref[...]))
    out = jnp.dot(w3_ref[0], h, precision=hp, preferred_element_type=F32)
    out = out * jnp.exp(-t * dl_ref[...])
    o_ref[...] = jnp.where(lag < seq_len, out, 0.0)


def _hyena_filter(f_w1, f_b1, f_freq, f_w2, f_b2, f_w3, seq_len, d):
    hid = f_w1.shape[1]
    bands = (FILTER_EMB - 1) // 2
    w1 = f_w1.astype(F32).T
    col = lambda v: v.astype(F32).reshape(-1, 1)
    freqs = jnp.linspace(1e-4, bands - 1, bands, dtype=F32).reshape(bands, 1)
    min_decay = math.log(DECAY_TARGET) / SLOW_DECAY_PCT
    max_decay = math.log(DECAY_TARGET) / FAST_DECAY_PCT
    deltas = jnp.abs(jnp.linspace(min_decay, max_decay, d, dtype=F32)).reshape(d, 1)
    w3 = f_w3.astype(F32).T.reshape(2, d, hid)
    full = lambda shape: pl.BlockSpec(shape, lambda s: (0,) * len(shape))
    return pl.pallas_call(
        functools.partial(_filter_kernel, seq_len=seq_len),
        grid=(2,),
        in_specs=[
            full((hid, 1)), full((hid, bands)), full((hid, bands)), full((hid, 1)), full((hid, 1)),
            full((hid, hid)), full((hid, 1)),
            pl.BlockSpec((1, d, hid), lambda s: (1 - s, 0, 0)),
            full((bands, 1)), full((d, 1)),
        ],
        out_specs=pl.BlockSpec((d, seq_len), lambda s: (0, s)),
        out_shape=jax.ShapeDtypeStruct((d, 2 * seq_len), F32),
        compiler_params=_params(("arbitrary",)),
        name="hyena_filter",
    )(w1[:, :1], w1[:, 1:1 + bands], w1[:, 1 + bands:], col(f_b1), col(f_freq),
      f_w2.astype(F32).T, col(f_b2), w3, freqs, deltas)


def _hyena_in_kernel(x_ref, w0_ref, w1_ref, w2_ref, b_ref, cw_ref, cb_ref, x0_ref, z_ref, xb_ref):
    c = pl.program_id(1)

    @pl.when(c == 0)
    def _():
        xb_ref[...] = x_ref[0].astype(BF16)

    xb = xb_ref[...]
    seq_len = xb.shape[0]
    row = lax.broadcasted_iota(I32, (seq_len, 1), 0)
    not_first = row > 0
    not_last = row < seq_len - 1

    def branch(w_ref, k):
        u = jnp.dot(xb, w_ref[...], preferred_element_type=F32) + b_ref[k]
        prev = jnp.where(not_first, pltpu.roll(u, 1, axis=0), 0.0)
        nxt = jnp.where(not_last, pltpu.roll(u, seq_len - 1, axis=0), 0.0)
        cw = cw_ref[k]
        return prev * cw[0:1] + u * cw[1:2] + nxt * cw[2:3] + cb_ref[k]

    x0_ref[0] = branch(w0_ref, 0).T
    z_ref[0] = (branch(w2_ref, 2) * branch(w1_ref, 1)).T


def _hyena_in(x3d, w_in, b_in, conv_w, conv_b):
    b, l, d = x3d.shape
    ch = HY_CH
    nc = d // ch
    b3 = b_in.astype(F32).reshape(3, 1, d)
    cw3 = conv_w.astype(F32).reshape(conv_w.shape[0], 3, d).transpose(1, 0, 2)
    cb3 = conv_b.astype(F32).reshape(3, 1, d)
    wspec = lambda k: pl.BlockSpec((d, ch), lambda i, c, k=k: (0, k * nc + c))
    return pl.pallas_call(
        _hyena_in_kernel,
        grid=(b, nc),
        in_specs=[
            pl.BlockSpec((1, l, d), lambda i, c: (i, 0, 0)),
            wspec(0), wspec(1), wspec(2),
            pl.BlockSpec((3, 1, ch), lambda i, c: (0, 0, c)),
            pl.BlockSpec((3, 3, ch), lambda i, c: (0, 0, c)),
            pl.BlockSpec((3, 1, ch), lambda i, c: (0, 0, c)),
        ],
        out_specs=[
            pl.BlockSpec((1, ch, l), lambda i, c: (i, c, 0)),
            pl.BlockSpec((1, ch, l), lambda i, c: (i, c, 0)),
        ],
        out_shape=[
            jax.ShapeDtypeStruct((b, d, l), F32),
            jax.ShapeDtypeStruct((b, d, l), F32),
        ],
        scratch_shapes=[pltpu.VMEM((l, d), BF16)],
        compiler_params=_params(("parallel", "arbitrary")),
        name="hyena_in_conv",
    )(x3d, w_in, w_in, w_in, b3, cw3, cb3)


def _long_conv_kernel(z_ref, x0_ref, kk_ref, skip_ref, o_ref, *, n_blk):
    blk = CONV_BLOCK
    nb = z_ref.shape[0]

    for c in range(z_ref.shape[1]):
        zc = z_ref[:, c, :]
        zr = jnp.concatenate([zc[:, j * blk:(j + 1) * blk] for j in range(n_blk)], axis=0).astype(BF16)
        kk = kk_ref[c:c + 1, :]
        acc = [None] * n_blk
        for delta in range(-(n_blk - 1), n_blk):
            start = (n_blk + delta - 1) * blk
            wrow = jnp.broadcast_to(kk[:, start:start + 2 * blk], (blk, 2 * blk))
            rolled = pltpu.roll(wrow, 0, axis=1, stride=1, stride_axis=0)
            tt = rolled[:, blk:].astype(BF16)
            j0, j1 = max(0, -delta), min(n_blk, n_blk - delta)
            res = jnp.dot(zr[j0 * nb:j1 * nb, :], tt, preferred_element_type=F32)
            for j in range(j0, j1):
                part = res[(j - j0) * nb:(j - j0 + 1) * nb, :]
                i = j + delta
                acc[i] = part if acc[i] is None else acc[i] + part
        y = jnp.concatenate(acc, axis=1)
        o_ref[:, c, :] = (y + zc * skip_ref[c:c + 1, :]) * x0_ref[:, c, :]


def _long_conv(z_t, x0_t, kk, skip):
    b, d, l = z_t.shape
    ch = CONV_CH
    kern = functools.partial(_long_conv_kernel, n_blk=l // CONV_BLOCK)
    return pl.pallas_call(
        kern,
        grid=(d // ch,),
        in_specs=[
            pl.BlockSpec((b, ch, l), lambda i: (0, i, 0)),
            pl.BlockSpec((b, ch, l), lambda i: (0, i, 0)),
            pl.BlockSpec((ch, 2 * l), lambda i: (i, 0)),
            pl.BlockSpec((ch, 1), lambda i: (i, 0)),
        ],
        out_specs=pl.BlockSpec((b, ch, l), lambda i: (0, i, 0)),
        out_shape=jax.ShapeDtypeStruct((b, d, l), F32),
        compiler_params=_params(("parallel",)),
        name="hyena_long_conv",
    )(z_t, x0_t, kk, skip)


def _rope_tables(seq_len, head_dim):
    rows = seq_len // GRID_W
    row = jnp.repeat(jnp.arange(rows), GRID_W)
    col = jnp.tile(jnp.arange(GRID_W), rows)
    axis_dim = head_dim // 2
    inv = ROPE_THETA ** (-jnp.arange(0, axis_dim, 2, dtype=F32) / axis_dim)
    ang = jnp.concatenate([row[:, None] * inv, col[:, None] * inv], -1)
    cos, sin = jnp.cos(ang), jnp.sin(ang)
    reps = LANES // head_dim
    cc = jnp.tile(jnp.concatenate([cos, cos], -1), (1, reps))
    ss = jnp.tile(jnp.concatenate([-sin, sin], -1), (1, reps))
    return cc, ss


def _moe(x1, aff, group_tokens, w_gate, w_up, w_down, layer, g, beta, alpha, split_rows=None):
    n, d = x1.shape
    ne = aff.shape[1]
    tt = TOK_TILE
    slots, gates, bases = [], [], []
    tok0, slot0 = 0, 0
    for ng in group_tokens:
        cap = EC_CAPACITY * ng // ne
        s, gt, bs = _select(aff[tok0:tok0 + ng].T.reshape(ne, ng // tt, tt), cap, slot0)
        slots.append(s)
        gates.append(gt)
        bases.append(bs[:, :, 0])
        tok0 += ng
        slot0 += cap
    cap_total = slot0
    slots_t = jnp.concatenate(slots, axis=1).transpose(1, 0, 2)
    gates_t = jnp.concatenate(gates, axis=1).transpose(1, 0, 2)
    base = jnp.concatenate(bases + [jnp.full((ne, 1), cap_total, I32)], axis=1).T
    meta = _routing_meta(base, cap_total)
    xe = _dispatch(x1, slots_t, meta, cap_total)
    ye = _expert_ffn(xe, w_gate, w_up, w_down, cap_total, layer)
    return _combine_ln(ye, slots_t, gates_t, meta, x1, g, beta, alpha, split_rows)


def kernel(x_prompt, x_sample, attn_w_qkv, attn_q_gain, attn_k_gain, attn_w_o, hy_w_in, hy_b_in, hy_conv_w, hy_conv_b, hy_f_w1, hy_f_b1, hy_f_freq, hy_f_w2, hy_f_b2, hy_f_w3, hy_skip, hy_w_out, hy_b_out, ln_mix_g, ln_mix_b, moe_router, moe_w_gate, moe_w_up, moe_w_down, ln_ffn_g, ln_ffn_b):
    bp, l, d = x_prompt.shape
    bs = x_sample.shape[0]
    assert x_sample.shape[1] == l
    depth = ln_mix_g.shape[0]
    alpha = (2 * depth) ** 0.25
    head_dim = d // N_HEADS
    b = bp + bs
    n = b * l
    group_tokens = (bp * l, bs * l)

    x = jnp.concatenate([x_prompt, x_sample], axis=0).reshape(n, d)
    cc, ss = _rope_tables(l, head_dim)
    seg = (lax.broadcasted_iota(I32, (LANES, LANES), 0) // head_dim
           == lax.broadcasted_iota(I32, (LANES, LANES), 1) // head_dim).astype(BF16)
    row = lambda v: v.astype(F32).reshape(1, -1)
    zero_bias = jnp.zeros((1, d), F32)
    stack = lambda w: w.astype(BF16).reshape((-1,) + w.shape[2:])
    w_gate, w_up, w_down = stack(moe_w_gate), stack(moe_w_up), stack(moe_w_down)

    for i in range(depth):
        j = i // 2
        r_hi, r_lo = _split_bf16(moe_router[i].astype(F32))
        if i % 2 == 0:
            gq = jnp.tile(row(attn_q_gain[j]), (1, LANES // head_dim))
            gk = jnp.tile(row(attn_k_gain[j]), (1, LANES // head_dim))
            q, k, v = _qkv_rope(x, attn_w_qkv[j].astype(BF16), cc, ss, gq, gk, seg, l)
            o = _attention(q.reshape(b, l, -1), k.reshape(b, l, -1), v.reshape(b, l, -1), tq=min(ATTN_STEP, l))
            mix_in, w_mix, b_mix = o.reshape(n, -1), attn_w_o[j].astype(BF16), zero_bias
        else:
            kk = _hyena_filter(hy_f_w1[j], hy_f_b1[j], hy_f_freq[j], hy_f_w2[j], hy_f_b2[j], hy_f_w3[j], l, d)
            x0, z = _hyena_in(x.reshape(b, l, d), hy_w_in[j].astype(BF16), hy_b_in[j], hy_conv_w[j], hy_conv_b[j])
            mix_in = _long_conv(z, x0, kk, hy_skip[j].astype(F32).reshape(d, 1))
            w_mix, b_mix = hy_w_out[j].astype(BF16), row(hy_b_out[j])
        x1, aff = _proj_ln_router(mix_in, w_mix, b_mix, x, row(ln_mix_g[i]), row(ln_mix_b[i]),
                                  r_hi, r_lo, alpha)
        last = i == depth - 1
        out = _moe(x1, aff, group_tokens, w_gate, w_up, w_down, i, row(ln_ffn_g[i]), row(ln_ffn_b[i]), alpha,
                   split_rows=group_tokens[0] if last else None)
        if last:
            return (out[0].reshape(bp, l, d), out[1].reshape(bs, l, d))
        x = out[0]
```

```python
import functools
import math

import jax
import jax.numpy as jnp
from jax import lax
from jax.experimental import pallas as pl
from jax.experimental.pallas import tpu as pltpu

F32 = jnp.float32
BF16 = jnp.bfloat16
I32 = jnp.int32

N_HEADS = 16
N_KV_HEADS = 4
GRID_W = 64
ROPE_THETA = 10000.0
QK_EPS = 1e-6
FILTER_EMB = 33
DECAY_TARGET = 1e-2
FAST_DECAY_PCT = 0.3
SLOW_DECAY_PCT = 1.5
EC_CAPACITY = 2
LN_EPS = 1e-5

LANES = 128
MXU_DIM = 256
VMEM_LIMIT_BYTES = 56 * 1024 * 1024

ROW_TILE = 512
PROJ_ROWS = 256
ATTN_ROWS = 256
ATTN_STEP = 512
TOK_TILE = 256
SELECT_EXPERTS = 4
SLOT_ALIGN = 16
SLOT_WIN = 64
FFN_ROWS = 1024
CONV_BLOCK = 256
CONV_CH = 8
HY_CH = 256


def _params(sem, vmem=VMEM_LIMIT_BYTES):
    return pltpu.CompilerParams(dimension_semantics=sem, vmem_limit_bytes=vmem)


def _layer_norm(v, g, b):
    mu = jnp.mean(v, axis=-1, keepdims=True)
    c = v - mu
    var = jnp.mean(c * c, axis=-1, keepdims=True)
    return c * lax.rsqrt(var + LN_EPS) * g + b


def _split_bf16(v):
    hi = v.astype(BF16)
    lo = (v - hi.astype(F32)).astype(BF16)
    return hi, lo


def _token_specs(xs, tm):
    d = xs[0].shape[1]
    if len(xs) == 1:
        return [pl.BlockSpec((tm, d), lambda i: (i, 0))], None
    first = xs[0].shape[0] // tm
    return [pl.BlockSpec((tm, d), lambda i: (jnp.minimum(i, first - 1), 0)),
            pl.BlockSpec((tm, d), lambda i: (jnp.maximum(i - first, 0), 0))], first


def _token_tile(x_refs, first_tiles):
    if first_tiles is None:
        return x_refs[0][...]
    return jnp.where(pl.program_id(0) < first_tiles, x_refs[0][...], x_refs[1][...])


def _qkv_kernel(*refs, n_x, first_tiles, n_q_chunks, n_k_chunks, head_dim, scale):
    w_ref, cc_ref, ss_ref, gq_ref, gk_ref, seg_ref, q_ref, k_ref, v_ref = refs[n_x:]
    xb = _token_tile(refs[:n_x], first_tiles).astype(BF16)
    qkv = jnp.dot(xb, w_ref[...], preferred_element_type=F32)
    cc = cc_ref[...]
    ss = ss_ref[...]
    seg = seg_ref[...]
    lane = lax.broadcasted_iota(I32, (xb.shape[0], LANES), 1)
    first_half = (lane % head_dim) < (head_dim // 2)
    half = head_dim // 2
    per = MXU_DIM // LANES
    rinv = []
    for c0 in range(0, n_q_chunks + n_k_chunks, per):
        u2 = qkv[:, c0 * LANES:(c0 + per) * LANES]
        s_hi, s_lo = _split_bf16(u2 * u2)
        ssum = (jnp.dot(s_hi, seg, preferred_element_type=F32)
                + jnp.dot(s_lo, seg, preferred_element_type=F32))
        r2 = lax.rsqrt(ssum * (1.0 / head_dim) + QK_EPS)
        rinv += [r2[:, h * LANES:(h + 1) * LANES] for h in range(per)]
    for c in range(n_q_chunks + n_k_chunks):
        u = qkv[:, c * LANES:(c + 1) * LANES]
        r = rinv[c]
        gain = gq_ref[...] if c < n_q_chunks else gk_ref[...]
        un = u * r * gain
        partner = jnp.where(first_half, pltpu.roll(un, LANES - half, axis=1), pltpu.roll(un, half, axis=1))
        o = un * cc + partner * ss
        if c < n_q_chunks:
            q_ref[:, c * LANES:(c + 1) * LANES] = (o * scale).astype(BF16)
        else:
            ck = c - n_q_chunks
            k_ref[:, ck * LANES:(ck + 1) * LANES] = o.astype(BF16)
    v_ref[...] = qkv[:, (n_q_chunks + n_k_chunks) * LANES:].astype(BF16)


def _qkv_rope(xs, w_qkv, cc, ss, gq, gk, seg, seq_len):
    n = sum(x.shape[0] for x in xs)
    d = xs[0].shape[1]
    head_dim = d // N_HEADS
    dq = N_HEADS * head_dim
    dk = N_KV_HEADS * head_dim
    tm = ROW_TILE
    tiles_per_seq = seq_len // tm
    x_specs, first_tiles = _token_specs(xs, tm)
    kern = functools.partial(_qkv_kernel, n_x=len(xs), first_tiles=first_tiles,
                             n_q_chunks=dq // LANES, n_k_chunks=dk // LANES,
                             head_dim=head_dim, scale=head_dim ** -0.5 * math.log2(math.e))
    return pl.pallas_call(
        kern,
        grid=(n // tm,),
        in_specs=x_specs + [
            pl.BlockSpec((d, dq + 2 * dk), lambda i: (0, 0)),
            pl.BlockSpec((tm, LANES), lambda i: (i % tiles_per_seq, 0)),
            pl.BlockSpec((tm, LANES), lambda i: (i % tiles_per_seq, 0)),
            pl.BlockSpec((1, LANES), lambda i: (0, 0)),
            pl.BlockSpec((1, LANES), lambda i: (0, 0)),
            pl.BlockSpec((MXU_DIM, MXU_DIM), lambda i: (0, 0)),
        ],
        out_specs=[
            pl.BlockSpec((tm, dq), lambda i: (i, 0)),
            pl.BlockSpec((tm, dk), lambda i: (i, 0)),
            pl.BlockSpec((tm, dk), lambda i: (i, 0)),
        ],
        out_shape=[
            jax.ShapeDtypeStruct((n, dq), BF16),
            jax.ShapeDtypeStruct((n, dk), BF16),
            jax.ShapeDtypeStruct((n, dk), BF16),
        ],
        compiler_params=_params(("parallel",)),
        name="attn_qkv_rope",
    )(*xs, w_qkv, cc, ss, gq, gk, seg)


def _attn_kernel(q_ref, k_ref, v_ref, o_ref, *, head_dim, group):
    tq = q_ref.shape[1]
    n_keys = k_ref.shape[1]
    lane = lax.broadcasted_iota(I32, (n_keys, head_dim), 1)
    ones_col = jnp.where(lane == 0, 1.0, 0.0).astype(BF16)
    for kv in range(N_KV_HEADS):
        k = k_ref[0, :, kv * head_dim:(kv + 1) * head_dim]
        v = v_ref[0, :, kv * head_dim:(kv + 1) * head_dim]
        v1 = jnp.concatenate([v, ones_col], axis=1)
        for r0 in range(0, tq, ATTN_ROWS):
            rows = slice(r0, r0 + ATTN_ROWS)
            qs = [q_ref[0, rows, (kv * group + g) * head_dim:(kv * group + g + 1) * head_dim]
                  for g in range(group)]
            q = jnp.concatenate(qs, axis=0)
            s = lax.dot_general(q, k, (((1,), (1,)), ((), ())), preferred_element_type=F32)
            m = jnp.max(s, axis=-1, keepdims=True)
            p = jnp.exp2(s - m).astype(BF16)
            ov = jnp.dot(p, v1, preferred_element_type=F32)
            o = ov[:, :head_dim] / ov[:, head_dim:head_dim + 1]
            outs = [o[g * ATTN_ROWS:(g + 1) * ATTN_ROWS, :] for g in range(group)]
            o_ref[0, rows, kv * group * head_dim:(kv + 1) * group * head_dim] = (
                jnp.concatenate(outs, axis=1).astype(BF16))


def _attention(q, k, v, tq):
    b, l, dq = q.shape
    dk = k.shape[2]
    head_dim = dq // N_HEADS
    kern = functools.partial(_attn_kernel, head_dim=head_dim, group=N_HEADS // N_KV_HEADS)
    return pl.pallas_call(
        kern,
        grid=(b, l // tq),
        in_specs=[
            pl.BlockSpec((1, tq, dq), lambda i, j: (i, j, 0)),
            pl.BlockSpec((1, l, dk), lambda i, j: (i, 0, 0)),
            pl.BlockSpec((1, l, dk), lambda i, j: (i, 0, 0)),
        ],
        out_specs=pl.BlockSpec((1, tq, dq), lambda i, j: (i, j, 0)),
        out_shape=jax.ShapeDtypeStruct((b, l, dq), BF16),
        compiler_params=_params(("parallel", "parallel")),
        name="attn_core",
    )(q, k, v)


def _proj_ln_router_kernel(*refs, n_x, first_tiles, alpha, channel_major):
    m_ref, w_ref, b_ref, g_ref, beta_ref, rc_ref, rh_ref, y_ref, aff_ref = refs[n_x:]
    m = m_ref[0].T.astype(BF16) if channel_major else m_ref[...]
    mix = jnp.dot(m, w_ref[...], preferred_element_type=F32) + b_ref[...]
    y = _layer_norm(alpha * _token_tile(refs[:n_x], first_tiles) + mix, g_ref[...], beta_ref[...])
    y_ref[...] = y
    y_hi, y_lo = _split_bf16(y)
    ne = rh_ref.shape[1]
    both = jnp.dot(y_hi, rc_ref[...], preferred_element_type=F32)
    lg = both[:, :ne] + both[:, ne:] + jnp.dot(y_lo, rh_ref[...], preferred_element_type=F32)
    ex = jnp.exp(lg - jnp.max(lg, axis=-1, keepdims=True))
    aff_ref[...] = ex / jnp.sum(ex, axis=-1, keepdims=True)


def _proj_ln_router(mix_in, w, bias, xs, g, beta, r_hi, r_lo, alpha):
    n = sum(x.shape[0] for x in xs)
    d = xs[0].shape[1]
    ne = r_hi.shape[1]
    tm = PROJ_ROWS
    channel_major = mix_in.ndim == 3
    if channel_major:
        dm = mix_in.shape[1]
        tiles_per_seq = mix_in.shape[2] // tm
        mix_spec = pl.BlockSpec((1, dm, tm), lambda i: (i // tiles_per_seq, 0, i % tiles_per_seq))
    else:
        dm = mix_in.shape[1]
        mix_spec = pl.BlockSpec((tm, dm), lambda i: (i, 0))
    x_specs, first_tiles = _token_specs(xs, tm)
    kern = functools.partial(_proj_ln_router_kernel, n_x=len(xs), first_tiles=first_tiles, alpha=alpha,
                             channel_major=channel_major)
    return pl.pallas_call(
        kern,
        grid=(n // tm,),
        in_specs=x_specs + [
            mix_spec,
            pl.BlockSpec((dm, d), lambda i: (0, 0)),
            pl.BlockSpec((1, d), lambda i: (0, 0)),
            pl.BlockSpec((1, d), lambda i: (0, 0)),
            pl.BlockSpec((1, d), lambda i: (0, 0)),
            pl.BlockSpec((d, 2 * ne), lambda i: (0, 0)),
            pl.BlockSpec((d, ne), lambda i: (0, 0)),
        ],
        out_specs=[
            pl.BlockSpec((tm, d), lambda i: (i, 0)),
            pl.BlockSpec((tm, ne), lambda i: (i, 0)),
        ],
        out_shape=[
            jax.ShapeDtypeStruct((n, d), F32),
            jax.ShapeDtypeStruct((n, ne), F32),
        ],
        compiler_params=_params(("parallel",)),
        name="proj_ln_router",
    )(*xs, mix_in, w, bias, g, beta, jnp.concatenate([r_hi, r_lo], axis=1), r_hi)


def _select_kernel(aff_ref, tri_ref, low_ref, slot_ref, gate_ref, base_ref, *, cap, slot_offset):
    n_exp, nt, tt = aff_ref.shape
    affs = [aff_ref[j] for j in range(n_exp)]
    bits = [pltpu.bitcast(a, I32) for a in affs]

    def count(mask):
        c = jnp.sum(mask.astype(F32), axis=1, keepdims=True)
        return jnp.sum(c, axis=0, keepdims=True)

    def step(i, prefixes):
        bit = jnp.left_shift(jnp.int32(1), 30 - i)
        cands = [p | bit for p in prefixes]
        return tuple(jnp.where(count(b >= c) >= cap, c, p) for b, c, p in zip(bits, cands, prefixes))

    thrs = lax.fori_loop(0, 31, step, tuple(jnp.zeros((1, 1), I32) for _ in range(n_exp)))

    tri = tri_ref[...]
    low = low_ref[...]
    ones = jnp.ones((tt, LANES), BF16)

    def excl_cumsum(mask):
        mf = mask.astype(F32).astype(BF16)
        within = jnp.dot(mf, tri, preferred_element_type=F32)
        tot = jnp.dot(mf, ones, preferred_element_type=F32).astype(BF16)
        base = jnp.dot(low, tot, preferred_element_type=F32)
        return within + base[:, :1], base

    for j in range(n_exp):
        gt = bits[j] > thrs[j]
        eq = bits[j] == thrs[j]
        need = cap - count(gt)
        rank_eq, _ = excl_cumsum(eq)
        sel = gt | (eq & (rank_eq < need))
        pos, base = excl_cumsum(sel)
        slot_ref[j] = jnp.where(sel, pos.astype(I32) + slot_offset, -1)
        gate_ref[j] = jnp.where(sel, affs[j], 0.0)
        base_ref[j] = base.astype(I32) + slot_offset


def _select(aff_t, cap, slot_offset):
    ne, nt, tt = aff_t.shape
    tri = (lax.broadcasted_iota(I32, (tt, tt), 0) < lax.broadcasted_iota(I32, (tt, tt), 1)).astype(BF16)
    low = (lax.broadcasted_iota(I32, (nt, nt), 0) > lax.broadcasted_iota(I32, (nt, nt), 1)).astype(BF16)
    kern = functools.partial(_select_kernel, cap=cap, slot_offset=slot_offset)
    eb = math.gcd(ne, SELECT_EXPERTS)
    return pl.pallas_call(
        kern,
        grid=(ne // eb,),
        in_specs=[
            pl.BlockSpec((eb, nt, tt), lambda e: (e, 0, 0)),
            pl.BlockSpec((tt, tt), lambda e: (0, 0)),
            pl.BlockSpec((nt, nt), lambda e: (0, 0)),
        ],
        out_specs=[
            pl.BlockSpec((eb, nt, tt), lambda e: (e, 0, 0)),
            pl.BlockSpec((eb, nt, tt), lambda e: (e, 0, 0)),
            pl.BlockSpec((eb, nt, LANES), lambda e: (e, 0, 0)),
        ],
        out_shape=[
            jax.ShapeDtypeStruct((ne, nt, tt), I32),
            jax.ShapeDtypeStruct((ne, nt, tt), F32),
            jax.ShapeDtypeStruct((ne, nt, LANES), I32),
        ],
        compiler_params=_params(("parallel",)),
        name="moe_select",
    )(aff_t, tri, low)


def _onehot_rows(slot_row, first_slot, n_rows):
    tt = slot_row.shape[1]
    rows = lax.broadcasted_iota(I32, (n_rows, tt), 0) + first_slot
    return rows == slot_row


def _routing_meta(base, cap_total):
    b0, b1 = base[:-1], base[1:]
    lower = (b0 // SLOT_ALIGN) * SLOT_ALIGN
    span = b1 - lower
    clamped = jnp.minimum(lower, cap_total - SLOT_WIN)
    rounds_dispatch = jnp.max(span // SLOT_WIN + 1, axis=1)
    rounds_combine = jnp.maximum(jnp.max((span + SLOT_WIN - 1) // SLOT_WIN, axis=1), 1)
    parts = [lower, b1, clamped, rounds_dispatch, rounds_combine]
    return jnp.concatenate([p.reshape(-1).astype(I32) for p in parts])


def _dispatch_kernel(meta_ref, x_ref, slot_ref, xe_ref, carry_ref, stage_ref, sem_ref, last_ref, *,
                     n_tiles, n_experts, cap_total, pad_windows):
    i = pl.program_id(0)

    def window_copy(e, start):
        rows = pl.ds(pl.multiple_of(start, SLOT_ALIGN), SLOT_WIN)
        return pltpu.make_async_copy(stage_ref.at[e], xe_ref.at[e, rows, :], sem_ref.at[e])

    @pl.when(i == 0)
    def _():
        carry_ref[...] = jnp.zeros_like(carry_ref)
        stage_ref[...] = jnp.zeros_like(stage_ref)
        pads = [window_copy(e, cap_total + k * SLOT_WIN) for e in range(n_experts) for k in range(pad_windows)]
        for cp in pads:
            cp.start()
        for cp in pads:
            cp.wait()

    xb = x_ref[...].astype(BF16)
    slots = slot_ref[0]

    def do_round(r, carry):
        starts = [meta_ref[i * n_experts + e] + r * SLOT_WIN for e in range(n_experts)]
        onehot = jnp.concatenate(
            [_onehot_rows(slots[e:e + 1, :], starts[e], SLOT_WIN) for e in range(n_experts)], axis=0)
        contrib = jnp.dot(onehot.astype(F32).astype(BF16), xb, preferred_element_type=F32)
        for e in range(n_experts):
            data = contrib[e * SLOT_WIN:(e + 1) * SLOT_WIN, :].astype(BF16)
            head = jnp.where(r == 0, carry_ref[e], jnp.zeros_like(carry_ref[e]))

            @pl.when(jnp.logical_or(i > 0, r > 0))
            def _(e=e):
                window_copy(e, last_ref[e]).wait()

            stage_ref[e, :SLOT_ALIGN, :] = data[:SLOT_ALIGN, :] + head
            stage_ref[e, SLOT_ALIGN:, :] = data[SLOT_ALIGN:, :]
            window_copy(e, starts[e]).start()
            last_ref[e] = starts[e]
            rem = meta_ref[(n_tiles + i) * n_experts + e] - starts[e]
            k = jnp.clip(rem // SLOT_ALIGN, 0, SLOT_WIN // SLOT_ALIGN - 1)
            tail = stage_ref[e, pl.ds(pl.multiple_of(k * SLOT_ALIGN, SLOT_ALIGN), SLOT_ALIGN), :]
            is_last = jnp.logical_and(rem >= 0, rem < SLOT_WIN)
            carry_ref[e] = jnp.where(is_last, tail, carry_ref[e])
        return carry

    lax.fori_loop(0, meta_ref[3 * n_tiles * n_experts + i], do_round, 0)

    @pl.when(i == n_tiles - 1)
    def _():
        for e in range(n_experts):
            window_copy(e, last_ref[e]).wait()


def _dispatch(x2d, slots_t, meta, cap_total):
    n, d = x2d.shape
    n_tiles, ne, tt = slots_t.shape
    pad_windows = tt // SLOT_WIN + 1
    pad_rows = pad_windows * SLOT_WIN
    kern = functools.partial(_dispatch_kernel, n_tiles=n_tiles, n_experts=ne, cap_total=cap_total,
                             pad_windows=pad_windows)
    return pl.pallas_call(
        kern,
        grid_spec=pltpu.PrefetchScalarGridSpec(
            num_scalar_prefetch=1,
            grid=(n_tiles,),
            in_specs=[
                pl.BlockSpec((tt, d), lambda i, b: (i, 0)),
                pl.BlockSpec((1, ne, tt), lambda i, b: (i, 0, 0)),
            ],
            out_specs=pl.BlockSpec(memory_space=pl.ANY),
            scratch_shapes=[
                pltpu.VMEM((ne, SLOT_ALIGN, d), BF16),
                pltpu.VMEM((ne, SLOT_WIN, d), BF16),
                pltpu.SemaphoreType.DMA((ne,)),
                pltpu.SMEM((ne,), I32),
            ],
        ),
        out_shape=jax.ShapeDtypeStruct((ne, cap_total + pad_rows, d), BF16),
        compiler_params=_params(("arbitrary",)),
        name="moe_dispatch",
    )(meta, x2d, slots_t)


def _ffn_kernel(x_ref, wg_ref, wu_ref, wd_ref, o_ref, *, chunk):
    x = x_ref[0]
    ff = wg_ref.shape[2]
    acc = None
    for c in range(ff // chunk):
        cols = slice(c * chunk, (c + 1) * chunk)
        hg = jnp.dot(x, wg_ref[0, :, cols], preferred_element_type=F32)
        hu = jnp.dot(x, wu_ref[0, :, cols], preferred_element_type=F32)
        h = (hg * jax.nn.sigmoid(hg) * hu).astype(BF16)
        part = jnp.dot(h, wd_ref[0, cols, :], preferred_element_type=F32)
        acc = part if acc is None else acc + part
    o_ref[0] = acc.astype(BF16)


def _expert_ffn(xe, wg, wu, wd, cap_total, layer):
    ne, _, d = xe.shape
    ff = wg.shape[2]
    tm = FFN_ROWS
    while cap_total % tm:
        tm //= 2
    chunk = MXU_DIM if ff % MXU_DIM == 0 else LANES
    w0 = layer * ne
    return pl.pallas_call(
        functools.partial(_ffn_kernel, chunk=chunk),
        grid=(ne, cap_total // tm),
        in_specs=[
            pl.BlockSpec((1, tm, d), lambda e, i: (e, i, 0)),
            pl.BlockSpec((1, d, ff), lambda e, i: (w0 + e, 0, 0)),
            pl.BlockSpec((1, d, ff), lambda e, i: (w0 + e, 0, 0)),
            pl.BlockSpec((1, ff, d), lambda e, i: (w0 + e, 0, 0)),
        ],
        out_specs=pl.BlockSpec((1, tm, d), lambda e, i: (e, i, 0)),
        out_shape=jax.ShapeDtypeStruct((ne, cap_total, d), BF16),
        compiler_params=_params(("parallel", "arbitrary")),
        name="moe_expert_ffn",
    )(xe, wg, wu, wd)


def _combine_kernel(meta_ref, *refs, n_tiles, n_experts, cap_total, alpha, split_tile):
    win_refs = refs[:n_experts]
    slot_ref, gate_ref, x_ref, g_ref, beta_ref, ye_ref = refs[n_experts:n_experts + 6]
    out_refs = refs[n_experts + 6:-2]
    extra_ref, sem_ref = refs[-2:]
    i = pl.program_id(0)
    slots = slot_ref[0]
    gates = gate_ref[0]

    te = n_tiles * n_experts

    def weights(lowers, starts):
        pieces = []
        for e in range(n_experts):
            srow = slots[e:e + 1, :]
            hit = jnp.logical_and(_onehot_rows(srow, starts[e], SLOT_WIN), srow >= lowers[e])
            pieces.append(jnp.where(hit, gates[e:e + 1, :], 0.0))
        return jnp.concatenate(pieces, axis=0).T.astype(BF16)

    lowers0 = [meta_ref[i * n_experts + e] for e in range(n_experts)]
    starts0 = [meta_ref[2 * te + i * n_experts + e] for e in range(n_experts)]
    ye0 = jnp.concatenate([r[0] for r in win_refs], axis=0)
    y = jnp.dot(weights(lowers0, starts0), ye0, preferred_element_type=F32)

    def extra_round(r, y):
        lowers = [lowers0[e] + r * SLOT_WIN for e in range(n_experts)]
        starts = [jnp.minimum(lowers[e], cap_total - SLOT_WIN) for e in range(n_experts)]
        for e in range(n_experts):
            rows = pl.ds(pl.multiple_of(starts[e], SLOT_ALIGN), SLOT_WIN)
            cp = pltpu.make_async_copy(ye_ref.at[e, rows, :],
                                       extra_ref.at[pl.ds(e * SLOT_WIN, SLOT_WIN), :], sem_ref.at[0])
            cp.start()
            cp.wait()
        return y + jnp.dot(weights(lowers, starts), extra_ref[...], preferred_element_type=F32)

    y = lax.fori_loop(1, meta_ref[3 * te + n_tiles + i], extra_round, y)
    out = _layer_norm(alpha * x_ref[...] + y, g_ref[...], beta_ref[...])
    if split_tile is None:
        out_refs[0][...] = out
    else:
        @pl.when(i < split_tile)
        def _():
            out_refs[0][...] = out

        @pl.when(i >= split_tile)
        def _():
            out_refs[1][...] = out


def _combine_ln(ye, slots_t, gates_t, meta, x2d, g, beta, alpha, split_rows=None):
    n, d = x2d.shape
    n_tiles, ne, tt = slots_t.shape
    cap_total = ye.shape[1]
    split_tile = None if split_rows is None else split_rows // tt
    kern = functools.partial(_combine_kernel, n_tiles=n_tiles, n_experts=ne, cap_total=cap_total, alpha=alpha,
                             split_tile=split_tile)

    def win_spec(e):
        def imap(i, meta):
            return (e, pl.multiple_of(meta[(2 * n_tiles + i) * ne + e], SLOT_ALIGN), 0)
        return pl.BlockSpec((pl.Element(1), pl.Element(SLOT_WIN), pl.Element(d)), imap)

    in_specs = [win_spec(e) for e in range(ne)]
    in_specs += [
        pl.BlockSpec((1, ne, tt), lambda i, b: (i, 0, 0)),
        pl.BlockSpec((1, ne, tt), lambda i, b: (i, 0, 0)),
        pl.BlockSpec((tt, d), lambda i, b: (i, 0)),
        pl.BlockSpec((1, d), lambda i, b: (0, 0)),
        pl.BlockSpec((1, d), lambda i, b: (0, 0)),
        pl.BlockSpec(memory_space=pl.ANY),
    ]
    if split_tile is None:
        out_specs = [pl.BlockSpec((tt, d), lambda i, b: (i, 0))]
        out_shape = [jax.ShapeDtypeStruct((n, d), F32)]
    else:
        out_specs = [pl.BlockSpec((tt, d), lambda i, b: (jnp.minimum(i, split_tile - 1), 0)),
                     pl.BlockSpec((tt, d), lambda i, b: (jnp.maximum(i - split_tile, 0), 0))]
        out_shape = [jax.ShapeDtypeStruct((split_rows, d), F32),
                     jax.ShapeDtypeStruct((n - split_rows, d), F32)]
    return pl.pallas_call(
        kern,
        grid_spec=pltpu.PrefetchScalarGridSpec(
            num_scalar_prefetch=1,
            grid=(n_tiles,),
            in_specs=in_specs,
            out_specs=out_specs,
            scratch_shapes=[
                pltpu.VMEM((ne * SLOT_WIN, d), BF16),
                pltpu.SemaphoreType.DMA((1,)),
            ],
        ),
        out_shape=out_shape,
        compiler_params=_params(("arbitrary",)),
        name="moe_combine_ln",
    )(meta, *([ye] * ne), slots_t, gates_t, x2d, g, beta, ye)


def _filter_kernel(w1t_ref, w1c_ref, w1s_ref, b1_ref, fq_ref, w2_ref, b2_ref, w3_ref, fr_ref, dl_ref,
                   o_ref, *, seq_len):
    half = pl.program_id(0)
    pos = lax.broadcasted_iota(I32, (1, seq_len), 1)
    lag = jnp.where(half == 0, seq_len - pos, pos)
    lagf = lag.astype(F32)
    t = lagf / (seq_len - 1.0)
    w = (2.0 * math.pi) * lagf / seq_len
    fw = fr_ref[...] * w
    hp = lax.Precision.HIGHEST
    pre = (w1t_ref[...] * t
           + jnp.dot(w1c_ref[...], jnp.cos(fw), precision=hp, preferred_element_type=F32)
           - jnp.dot(w1s_ref[...], jnp.sin(fw), precision=hp, preferred_element_type=F32)
           + b1_ref[...])
    fq = fq_ref[...]
    h = jnp.sin(fq * pre)
    h = jnp.sin(fq * (jnp.dot(w2_ref[...], h, precision=hp, preferred_element_type=F32) + b2_ref[...]))
    out = jnp.dot(w3_ref[0], h, precision=hp, preferred_element_type=F32)
    out = out * jnp.exp(-t * dl_ref[...])
    o_ref[...] = jnp.where(lag < seq_len, out, 0.0)


def _hyena_filter(f_w1, f_b1, f_freq, f_w2, f_b2, f_w3, seq_len, d):
    hid = f_w1.shape[1]
    bands = (FILTER_EMB - 1) // 2
    w1 = f_w1.astype(F32).T
    col = lambda v: v.astype(F32).reshape(-1, 1)
    freqs = jnp.linspace(1e-4, bands - 1, bands, dtype=F32).reshape(bands, 1)
    min_decay = math.log(DECAY_TARGET) / SLOW_DECAY_PCT
    max_decay = math.log(DECAY_TARGET) / FAST_DECAY_PCT
    deltas = jnp.abs(jnp.linspace(min_decay, max_decay, d, dtype=F32)).reshape(d, 1)
    w3 = f_w3.astype(F32).T.reshape(2, d, hid)
    full = lambda shape: pl.BlockSpec(shape, lambda s: (0,) * len(shape))
    return pl.pallas_call(
        functools.partial(_filter_kernel, seq_len=seq_len),
        grid=(2,),
        in_specs=[
            full((hid, 1)), full((hid, bands)), full((hid, bands)), full((hid, 1)), full((hid, 1)),
            full((hid, hid)), full((hid, 1)),
            pl.BlockSpec((1, d, hid), lambda s: (1 - s, 0, 0)),
            full((bands, 1)), full((d, 1)),
        ],
        out_specs=pl.BlockSpec((d, seq_len), lambda s: (0, s)),
        out_shape=jax.ShapeDtypeStruct((d, 2 * seq_len), F32),
        compiler_params=_params(("arbitrary",)),
        name="hyena_filter",
    )(w1[:, :1], w1[:, 1:1 + bands], w1[:, 1 + bands:], col(f_b1), col(f_freq),
      f_w2.astype(F32).T, col(f_b2), w3, freqs, deltas)


def _hyena_in_kernel(x_ref, w0_ref, w1_ref, w2_ref, b_ref, cw_ref, cb_ref, x0_ref, z_ref, xb_ref):
    c = pl.program_id(1)

    @pl.when(c == 0)
    def _():
        xb_ref[...] = x_ref[0].astype(BF16)

    xb = xb_ref[...]
    seq_len = xb.shape[0]
    row = lax.broadcasted_iota(I32, (seq_len, 1), 0)
    not_first = row > 0
    not_last = row < seq_len - 1

    def branch(w_ref, k):
        u = jnp.dot(xb, w_ref[...], preferred_element_type=F32) + b_ref[k]
        prev = jnp.where(not_first, pltpu.roll(u, 1, axis=0), 0.0)
        nxt = jnp.where(not_last, pltpu.roll(u, seq_len - 1, axis=0), 0.0)
        cw = cw_ref[k]
        return prev * cw[0:1] + u * cw[1:2] + nxt * cw[2:3] + cb_ref[k]

    x0_ref[0] = branch(w0_ref, 0).T
    z_ref[0] = (branch(w2_ref, 2) * branch(w1_ref, 1)).T


def _hyena_in(x3d, w_in, b_in, conv_w, conv_b):
    b, l, d = x3d.shape
    ch = HY_CH
    nc = d // ch
    b3 = b_in.astype(F32).reshape(3, 1, d)
    cw3 = conv_w.astype(F32).reshape(conv_w.shape[0], 3, d).transpose(1, 0, 2)
    cb3 = conv_b.astype(F32).reshape(3, 1, d)
    wspec = lambda k: pl.BlockSpec((d, ch), lambda i, c, k=k: (0, k * nc + c))
    return pl.pallas_call(
        _hyena_in_kernel,
        grid=(b, nc),
        in_specs=[
            pl.BlockSpec((1, l, d), lambda i, c: (i, 0, 0)),
            wspec(0), wspec(1), wspec(2),
            pl.BlockSpec((3, 1, ch), lambda i, c: (0, 0, c)),
            pl.BlockSpec((3, 3, ch), lambda i, c: (0, 0, c)),
            pl.BlockSpec((3, 1, ch), lambda i, c: (0, 0, c)),
        ],
        out_specs=[
            pl.BlockSpec((1, ch, l), lambda i, c: (i, c, 0)),
            pl.BlockSpec((1, ch, l), lambda i, c: (i, c, 0)),
        ],
        out_shape=[
            jax.ShapeDtypeStruct((b, d, l), F32),
            jax.ShapeDtypeStruct((b, d, l), F32),
        ],
        scratch_shapes=[pltpu.VMEM((l, d), BF16)],
        compiler_params=_params(("parallel", "arbitrary")),
        name="hyena_in_conv",
    )(x3d, w_in, w_in, w_in, b3, cw3, cb3)


def _long_conv_kernel(z_ref, x0_ref, kk_ref, skip_ref, o_ref, *, n_blk):
    blk = CONV_BLOCK
    nb = z_ref.shape[0]

    for c in range(z_ref.shape[1]):
        zc = z_ref[:, c, :]
        zr = jnp.concatenate([zc[:, j * blk:(j + 1) * blk] for j in range(n_blk)], axis=0).astype(BF16)
        kk = kk_ref[c:c + 1, :]
        acc = [None] * n_blk
        for delta in range(-(n_blk - 1), n_blk):
            start = (n_blk + delta - 1) * blk
            wrow = jnp.broadcast_to(kk[:, start:start + 2 * blk], (blk, 2 * blk))
            rolled = pltpu.roll(wrow, 0, axis=1, stride=1, stride_axis=0)
            tt = rolled[:, blk:].astype(BF16)
            j0, j1 = max(0, -delta), min(n_blk, n_blk - delta)
            res = jnp.dot(zr[j0 * nb:j1 * nb, :], tt, preferred_element_type=F32)
            for j in range(j0, j1):
                part = res[(j - j0) * nb:(j - j0 + 1) * nb, :]
                i = j + delta
                acc[i] = part if acc[i] is None else acc[i] + part
        y = jnp.concatenate(acc, axis=1)
        o_ref[:, c, :] = (y + zc * skip_ref[c:c + 1, :]) * x0_ref[:, c, :]


def _long_conv(z_t, x0_t, kk, skip):
    b, d, l = z_t.shape
    ch = CONV_CH
    kern = functools.partial(_long_conv_kernel, n_blk=l // CONV_BLOCK)
    return pl.pallas_call(
        kern,
        grid=(d // ch,),
        in_specs=[
            pl.BlockSpec((b, ch, l), lambda i: (0, i, 0)),
            pl.BlockSpec((b, ch, l), lambda i: (0, i, 0)),
            pl.BlockSpec((ch, 2 * l), lambda i: (i, 0)),
            pl.BlockSpec((ch, 1), lambda i: (i, 0)),
        ],
        out_specs=pl.BlockSpec((b, ch, l), lambda i: (0, i, 0)),
        out_shape=jax.ShapeDtypeStruct((b, d, l), F32),
        compiler_params=_params(("parallel",)),
        name="hyena_long_conv",
    )(z_t, x0_t, kk, skip)


def _rope_tables(seq_len, head_dim):
    rows = seq_len // GRID_W
    row = jnp.repeat(jnp.arange(rows), GRID_W)
    col = jnp.tile(jnp.arange(GRID_W), rows)
    axis_dim = head_dim // 2
    inv = ROPE_THETA ** (-jnp.arange(0, axis_dim, 2, dtype=F32) / axis_dim)
    ang = jnp.concatenate([row[:, None] * inv, col[:, None] * inv], -1)
    cos, sin = jnp.cos(ang), jnp.sin(ang)
    reps = LANES // head_dim
    cc = jnp.tile(jnp.concatenate([cos, cos], -1), (1, reps))
    ss = jnp.tile(jnp.concatenate([-sin, sin], -1), (1, reps))
    return cc, ss


def _moe(x1, aff, group_tokens, w_gate, w_up, w_down, layer, g, beta, alpha, split_rows=None):
    n, d = x1.shape
    ne = aff.shape[1]
    tt = TOK_TILE
    slots, gates, bases = [], [], []
    tok0, slot0 = 0, 0
    for ng in group_tokens:
        cap = EC_CAPACITY * ng // ne
        s, gt, bs = _select(aff[tok0:tok0 + ng].T.reshape(ne, ng // tt, tt), cap, slot0)
        slots.append(s)
        gates.append(gt)
        bases.append(bs[:, :, 0])
        tok0 += ng
        slot0 += cap
    cap_total = slot0
    slots_t = jnp.concatenate(slots, axis=1).transpose(1, 0, 2)
    gates_t = jnp.concatenate(gates, axis=1).transpose(1, 0, 2)
    base = jnp.concatenate(bases + [jnp.full((ne, 1), cap_total, I32)], axis=1).T
    meta = _routing_meta(base, cap_total)
    xe = _dispatch(x1, slots_t, meta, cap_total)
    ye = _expert_ffn(xe, w_gate, w_up, w_down, cap_total, layer)
    return _combine_ln(ye, slots_t, gates_t, meta, x1, g, beta, alpha, split_rows)


def kernel(x_prompt, x_sample, attn_w_qkv, attn_q_gain, attn_k_gain, attn_w_o, hy_w_in, hy_b_in, hy_conv_w, hy_conv_b, hy_f_w1, hy_f_b1, hy_f_freq, hy_f_w2, hy_f_b2, hy_f_w3, hy_skip, hy_w_out, hy_b_out, ln_mix_g, ln_mix_b, moe_router, moe_w_gate, moe_w_up, moe_w_down, ln_ffn_g, ln_ffn_b):
    bp, l, d = x_prompt.shape
    bs = x_sample.shape[0]
    assert x_sample.shape[1] == l
    depth = ln_mix_g.shape[0]
    alpha = (2 * depth) ** 0.25
    head_dim = d // N_HEADS
    b = bp + bs
    n = b * l
    group_tokens = (bp * l, bs * l)

    xs = [x_prompt.reshape(bp * l, d), x_sample.reshape(bs * l, d)]
    cc, ss = _rope_tables(l, head_dim)
    seg = (lax.broadcasted_iota(I32, (MXU_DIM, MXU_DIM), 0) // head_dim
           == lax.broadcasted_iota(I32, (MXU_DIM, MXU_DIM), 1) // head_dim).astype(BF16)
    row = lambda v: v.astype(F32).reshape(1, -1)
    zero_bias = jnp.zeros((1, d), F32)
    stack = lambda w: w.astype(BF16).reshape((-1,) + w.shape[2:])
    w_gate, w_up, w_down = stack(moe_w_gate), stack(moe_w_up), stack(moe_w_down)

    for i in range(depth):
        j = i // 2
        r_hi, r_lo = _split_bf16(moe_router[i].astype(F32))
        if i % 2 == 0:
            gq = jnp.tile(row(attn_q_gain[j]), (1, LANES // head_dim))
            gk = jnp.tile(row(attn_k_gain[j]), (1, LANES // head_dim))
            q, k, v = _qkv_rope(xs, attn_w_qkv[j].astype(BF16), cc, ss, gq, gk, seg, l)
            o = _attention(q.reshape(b, l, -1), k.reshape(b, l, -1), v.reshape(b, l, -1), tq=min(ATTN_STEP, l))
            mix_in, w_mix, b_mix = o.reshape(n, -1), attn_w_o[j].astype(BF16), zero_bias
        else:
            kk = _hyena_filter(hy_f_w1[j], hy_f_b1[j], hy_f_freq[j], hy_f_w2[j], hy_f_b2[j], hy_f_w3[j], l, d)
            (x,) = xs
            x0, z = _hyena_in(x.reshape(b, l, d), hy_w_in[j].astype(BF16), hy_b_in[j], hy_conv_w[j], hy_conv_b[j])
            mix_in = _long_conv(z, x0, kk, hy_skip[j].astype(F32).reshape(d, 1))
            w_mix, b_mix = hy_w_out[j].astype(BF16), row(hy_b_out[j])
        x1, aff = _proj_ln_router(mix_in, w_mix, b_mix, xs, row(ln_mix_g[i]), row(ln_mix_b[i]),
                                  r_hi, r_lo, alpha)
        last = i == depth - 1
        out = _moe(x1, aff, group_tokens, w_gate, w_up, w_down, i, row(ln_ffn_g[i]), row(ln_ffn_b[i]), alpha,
                   split_rows=group_tokens[0] if last else None)
        if last:
            return (out[0].reshape(bp, l, d), out[1].reshape(bs, l, d))
        xs = [out[0]]
```

```python
import functools
import math

import jax
import jax.numpy as jnp
from jax import lax
from jax.experimental import pallas as pl
from jax.experimental.pallas import tpu as pltpu

F32 = jnp.float32
BF16 = jnp.bfloat16
I32 = jnp.int32

N_HEADS = 16
N_KV_HEADS = 4
GRID_W = 64
ROPE_THETA = 10000.0
QK_EPS = 1e-6
FILTER_EMB = 33
DECAY_TARGET = 1e-2
FAST_DECAY_PCT = 0.3
SLOW_DECAY_PCT = 1.5
EC_CAPACITY = 2
LN_EPS = 1e-5

LANES = 128
MXU_DIM = 256
VMEM_LIMIT_BYTES = 56 * 1024 * 1024

ROW_TILE = 512
PROJ_ROWS = 256
ATTN_ROWS = 256
ATTN_STEP = 512
TOK_TILE = 256
SELECT_EXPERTS = 4
SLOT_ALIGN = 16
SLOT_WIN = 64
FFN_ROWS = 1024
CONV_BLOCK = 256
CONV_CH = 8
HY_CH = 256


def _params(sem, vmem=VMEM_LIMIT_BYTES):
    return pltpu.CompilerParams(dimension_semantics=sem, vmem_limit_bytes=vmem)


def _layer_norm(v, g, b):
    mu = jnp.mean(v, axis=-1, keepdims=True)
    c = v - mu
    var = jnp.mean(c * c, axis=-1, keepdims=True)
    return c * lax.rsqrt(var + LN_EPS) * g + b


def _split_bf16(v):
    hi = v.astype(BF16)
    lo = (v - hi.astype(F32)).astype(BF16)
    return hi, lo


def _token_specs(xs, tm):
    d = xs[0].shape[1]
    if len(xs) == 1:
        return [pl.BlockSpec((tm, d), lambda i: (i, 0))], None
    first = xs[0].shape[0] // tm
    return [pl.BlockSpec((tm, d), lambda i: (jnp.minimum(i, first - 1), 0)),
            pl.BlockSpec((tm, d), lambda i: (jnp.maximum(i - first, 0), 0))], first


def _token_tile(x_refs, first_tiles):
    if first_tiles is None:
        return x_refs[0][...]
    return jnp.where(pl.program_id(0) < first_tiles, x_refs[0][...], x_refs[1][...])


def _qkv_kernel(*refs, n_x, first_tiles, n_q_chunks, n_k_chunks, head_dim, scale):
    w_ref, cc_ref, ss_ref, gq_ref, gk_ref, seg_ref, q_ref, k_ref, v_ref = refs[n_x:]
    xb = _token_tile(refs[:n_x], first_tiles).astype(BF16)
    qkv = jnp.dot(xb, w_ref[...], preferred_element_type=F32)
    cc = cc_ref[...]
    ss = ss_ref[...]
    seg = seg_ref[...]
    lane = lax.broadcasted_iota(I32, (xb.shape[0], LANES), 1)
    first_half = (lane % head_dim) < (head_dim // 2)
    half = head_dim // 2
    per = MXU_DIM // LANES
    rinv = []
    for c0 in range(0, n_q_chunks + n_k_chunks, per):
        u2 = qkv[:, c0 * LANES:(c0 + per) * LANES]
        s_hi, s_lo = _split_bf16(u2 * u2)
        ssum = (jnp.dot(s_hi, seg, preferred_element_type=F32)
                + jnp.dot(s_lo, seg, preferred_element_type=F32))
        r2 = lax.rsqrt(ssum * (1.0 / head_dim) + QK_EPS)
        rinv += [r2[:, h * LANES:(h + 1) * LANES] for h in range(per)]
    for c in range(n_q_chunks + n_k_chunks):
        u = qkv[:, c * LANES:(c + 1) * LANES]
        r = rinv[c]
        gain = gq_ref[...] if c < n_q_chunks else gk_ref[...]
        un = u * r * gain
        partner = jnp.where(first_half, pltpu.roll(un, LANES - half, axis=1), pltpu.roll(un, half, axis=1))
        o = un * cc + partner * ss
        if c < n_q_chunks:
            q_ref[:, c * LANES:(c + 1) * LANES] = (o * scale).astype(BF16)
        else:
            ck = c - n_q_chunks
            k_ref[:, ck * LANES:(ck + 1) * LANES] = o.astype(BF16)
    v_ref[...] = qkv[:, (n_q_chunks + n_k_chunks) * LANES:].astype(BF16)


def _qkv_rope(xs, w_qkv, cc, ss, gq, gk, seg, seq_len):
    n = sum(x.shape[0] for x in xs)
    d = xs[0].shape[1]
    head_dim = d // N_HEADS
    dq = N_HEADS * head_dim
    dk = N_KV_HEADS * head_dim
    tm = ROW_TILE
    tiles_per_seq = seq_len // tm
    x_specs, first_tiles = _token_specs(xs, tm)
    kern = functools.partial(_qkv_kernel, n_x=len(xs), first_tiles=first_tiles,
                             n_q_chunks=dq // LANES, n_k_chunks=dk // LANES,
                             head_dim=head_dim, scale=head_dim ** -0.5 * math.log2(math.e))
    return pl.pallas_call(
        kern,
        grid=(n // tm,),
        in_specs=x_specs + [
            pl.BlockSpec((d, dq + 2 * dk), lambda i: (0, 0)),
            pl.BlockSpec((tm, LANES), lambda i: (i % tiles_per_seq, 0)),
            pl.BlockSpec((tm, LANES), lambda i: (i % tiles_per_seq, 0)),
            pl.BlockSpec((1, LANES), lambda i: (0, 0)),
            pl.BlockSpec((1, LANES), lambda i: (0, 0)),
            pl.BlockSpec((MXU_DIM, MXU_DIM), lambda i: (0, 0)),
        ],
        out_specs=[
            pl.BlockSpec((tm, dq), lambda i: (i, 0)),
            pl.BlockSpec((tm, dk), lambda i: (i, 0)),
            pl.BlockSpec((tm, dk), lambda i: (i, 0)),
        ],
        out_shape=[
            jax.ShapeDtypeStruct((n, dq), BF16),
            jax.ShapeDtypeStruct((n, dk), BF16),
            jax.ShapeDtypeStruct((n, dk), BF16),
        ],
        compiler_params=_params(("parallel",)),
        name="attn_qkv_rope",
    )(*xs, w_qkv, cc, ss, gq, gk, seg)


def _attn_kernel(q_ref, k_ref, v_ref, o_ref, *, head_dim, group):
    tq = q_ref.shape[1]
    n_keys = k_ref.shape[1]
    lane = lax.broadcasted_iota(I32, (n_keys, head_dim), 1)
    ones_col = jnp.where(lane == 0, 1.0, 0.0).astype(BF16)
    for kv in range(N_KV_HEADS):
        k = k_ref[0, :, kv * head_dim:(kv + 1) * head_dim]
        v = v_ref[0, :, kv * head_dim:(kv + 1) * head_dim]
        v1 = jnp.concatenate([v, ones_col], axis=1)
        for r0 in range(0, tq, ATTN_ROWS):
            rows = slice(r0, r0 + ATTN_ROWS)
            qs = [q_ref[0, rows, (kv * group + g) * head_dim:(kv * group + g + 1) * head_dim]
                  for g in range(group)]
            q = jnp.concatenate(qs, axis=0)
            s = lax.dot_general(q, k, (((1,), (1,)), ((), ())), preferred_element_type=F32)
            m = jnp.max(s, axis=-1, keepdims=True)
            p = jnp.exp2(s - m).astype(BF16)
            ov = jnp.dot(p, v1, preferred_element_type=F32)
            o = ov[:, :head_dim] / ov[:, head_dim:head_dim + 1]
            outs = [o[g * ATTN_ROWS:(g + 1) * ATTN_ROWS, :] for g in range(group)]
            o_ref[0, rows, kv * group * head_dim:(kv + 1) * group * head_dim] = (
                jnp.concatenate(outs, axis=1).astype(BF16))


def _attention(q, k, v, tq):
    b, l, dq = q.shape
    dk = k.shape[2]
    head_dim = dq // N_HEADS
    kern = functools.partial(_attn_kernel, head_dim=head_dim, group=N_HEADS // N_KV_HEADS)
    return pl.pallas_call(
        kern,
        grid=(b, l // tq),
        in_specs=[
            pl.BlockSpec((1, tq, dq), lambda i, j: (i, j, 0)),
            pl.BlockSpec((1, l, dk), lambda i, j: (i, 0, 0)),
            pl.BlockSpec((1, l, dk), lambda i, j: (i, 0, 0)),
        ],
        out_specs=pl.BlockSpec((1, tq, dq), lambda i, j: (i, j, 0)),
        out_shape=jax.ShapeDtypeStruct((b, l, dq), BF16),
        compiler_params=_params(("parallel", "parallel")),
        name="attn_core",
    )(q, k, v)


def _proj_ln_router_kernel(*refs, n_x, first_tiles, alpha, channel_major):
    m_ref, w_ref, b_ref, g_ref, beta_ref, rc_ref, rh_ref, y_ref, aff_ref = refs[n_x:]
    m = m_ref[0].T.astype(BF16) if channel_major else m_ref[...]
    mix = jnp.dot(m, w_ref[...], preferred_element_type=F32) + b_ref[...]
    y = _layer_norm(alpha * _token_tile(refs[:n_x], first_tiles) + mix, g_ref[...], beta_ref[...])
    y_ref[...] = y
    y_hi, y_lo = _split_bf16(y)
    ne = rh_ref.shape[1]
    both = jnp.dot(y_hi, rc_ref[...], preferred_element_type=F32)
    lg = both[:, :ne] + both[:, ne:] + jnp.dot(y_lo, rh_ref[...], preferred_element_type=F32)
    ex = jnp.exp(lg - jnp.max(lg, axis=-1, keepdims=True))
    aff_ref[...] = ex / jnp.sum(ex, axis=-1, keepdims=True)


def _proj_ln_router(mix_in, w, bias, xs, g, beta, r_hi, r_lo, alpha):
    n = sum(x.shape[0] for x in xs)
    d = xs[0].shape[1]
    ne = r_hi.shape[1]
    tm = PROJ_ROWS
    channel_major = mix_in.ndim == 3
    if channel_major:
        dm = mix_in.shape[1]
        tiles_per_seq = mix_in.shape[2] // tm
        mix_spec = pl.BlockSpec((1, dm, tm), lambda i: (i // tiles_per_seq, 0, i % tiles_per_seq))
    else:
        dm = mix_in.shape[1]
        mix_spec = pl.BlockSpec((tm, dm), lambda i: (i, 0))
    x_specs, first_tiles = _token_specs(xs, tm)
    kern = functools.partial(_proj_ln_router_kernel, n_x=len(xs), first_tiles=first_tiles, alpha=alpha,
                             channel_major=channel_major)
    return pl.pallas_call(
        kern,
        grid=(n // tm,),
        in_specs=x_specs + [
            mix_spec,
            pl.BlockSpec((dm, d), lambda i: (0, 0)),
            pl.BlockSpec((1, d), lambda i: (0, 0)),
            pl.BlockSpec((1, d), lambda i: (0, 0)),
            pl.BlockSpec((1, d), lambda i: (0, 0)),
            pl.BlockSpec((d, 2 * ne), lambda i: (0, 0)),
            pl.BlockSpec((d, ne), lambda i: (0, 0)),
        ],
        out_specs=[
            pl.BlockSpec((tm, d), lambda i: (i, 0)),
            pl.BlockSpec((tm, ne), lambda i: (i, 0)),
        ],
        out_shape=[
            jax.ShapeDtypeStruct((n, d), F32),
            jax.ShapeDtypeStruct((n, ne), F32),
        ],
        compiler_params=_params(("parallel",)),
        name="proj_ln_router",
    )(*xs, mix_in, w, bias, g, beta, jnp.concatenate([r_hi, r_lo], axis=1), r_hi)


def _select_kernel(aff_ref, tri_ref, low_ref, slot_ref, gate_ref, base_ref, *, cap, slot_offset):
    n_exp, nt, tt = aff_ref.shape
    affs = [aff_ref[j] for j in range(n_exp)]
    bits = [pltpu.bitcast(a, I32) for a in affs]

    def count(mask):
        c = jnp.sum(mask.astype(F32), axis=1, keepdims=True)
        return jnp.sum(c, axis=0, keepdims=True)

    def step(i, prefixes):
        bit = jnp.left_shift(jnp.int32(1), 30 - i)
        cands = [p | bit for p in prefixes]
        return tuple(jnp.where(count(b >= c) >= cap, c, p) for b, c, p in zip(bits, cands, prefixes))

    thrs = lax.fori_loop(0, 31, step, tuple(jnp.zeros((1, 1), I32) for _ in range(n_exp)))

    tri = tri_ref[...]
    low = low_ref[...]
    ones = jnp.ones((tt, LANES), BF16)

    def excl_cumsum(mask):
        mf = mask.astype(F32).astype(BF16)
        within = jnp.dot(mf, tri, preferred_element_type=F32)
        tot = jnp.dot(mf, ones, preferred_element_type=F32).astype(BF16)
        base = jnp.dot(low, tot, preferred_element_type=F32)
        return within + base[:, :1], base

    for j in range(n_exp):
        gt = bits[j] > thrs[j]
        eq = bits[j] == thrs[j]
        need = cap - count(gt)
        rank_eq, _ = excl_cumsum(eq)
        sel = gt | (eq & (rank_eq < need))
        pos, base = excl_cumsum(sel)
        slot_ref[j] = jnp.where(sel, pos.astype(I32) + slot_offset, -1)
        gate_ref[j] = jnp.where(sel, affs[j], 0.0)
        base_ref[j] = base.astype(I32) + slot_offset


def _select(aff_t, cap, slot_offset):
    ne, nt, tt = aff_t.shape
    tri = (lax.broadcasted_iota(I32, (tt, tt), 0) < lax.broadcasted_iota(I32, (tt, tt), 1)).astype(BF16)
    low = (lax.broadcasted_iota(I32, (nt, nt), 0) > lax.broadcasted_iota(I32, (nt, nt), 1)).astype(BF16)
    kern = functools.partial(_select_kernel, cap=cap, slot_offset=slot_offset)
    eb = math.gcd(ne, SELECT_EXPERTS)
    return pl.pallas_call(
        kern,
        grid=(ne // eb,),
        in_specs=[
            pl.BlockSpec((eb, nt, tt), lambda e: (e, 0, 0)),
            pl.BlockSpec((tt, tt), lambda e: (0, 0)),
            pl.BlockSpec((nt, nt), lambda e: (0, 0)),
        ],
        out_specs=[
            pl.BlockSpec((eb, nt, tt), lambda e: (e, 0, 0)),
            pl.BlockSpec((eb, nt, tt), lambda e: (e, 0, 0)),
            pl.BlockSpec((eb, nt, LANES), lambda e: (e, 0, 0)),
        ],
        out_shape=[
            jax.ShapeDtypeStruct((ne, nt, tt), I32),
            jax.ShapeDtypeStruct((ne, nt, tt), F32),
            jax.ShapeDtypeStruct((ne, nt, LANES), I32),
        ],
        compiler_params=_params(("parallel",)),
        name="moe_select",
    )(aff_t, tri, low)


def _onehot_rows(slot_row, first_slot, n_rows):
    tt = slot_row.shape[1]
    rows = lax.broadcasted_iota(I32, (n_rows, tt), 0) + first_slot
    return rows == slot_row


def _routing_meta(base, cap_total):
    b0, b1 = base[:-1], base[1:]
    lower = (b0 // SLOT_ALIGN) * SLOT_ALIGN
    span = b1 - lower
    clamped = jnp.minimum(lower, cap_total - SLOT_WIN)
    rounds_dispatch = jnp.max(span // SLOT_WIN + 1, axis=1)
    rounds_combine = jnp.maximum(jnp.max((span + SLOT_WIN - 1) // SLOT_WIN, axis=1), 1)
    parts = [lower, b1, clamped, rounds_dispatch, rounds_combine]
    return jnp.concatenate([p.reshape(-1).astype(I32) for p in parts])


def _dispatch_kernel(meta_ref, x_ref, slot_ref, xe_ref, carry_ref, stage_ref, sem_ref, last_ref, *,
                     n_tiles, n_experts, cap_total, pad_windows):
    i = pl.program_id(0)

    def window_copy(e, start):
        rows = pl.ds(pl.multiple_of(start, SLOT_ALIGN), SLOT_WIN)
        return pltpu.make_async_copy(stage_ref.at[e], xe_ref.at[e, rows, :], sem_ref.at[e])

    @pl.when(i == 0)
    def _():
        carry_ref[...] = jnp.zeros_like(carry_ref)
        stage_ref[...] = jnp.zeros_like(stage_ref)
        pads = [window_copy(e, cap_total + k * SLOT_WIN) for e in range(n_experts) for k in range(pad_windows)]
        for cp in pads:
            cp.start()
        for cp in pads:
            cp.wait()

    xb = x_ref[...].astype(BF16)
    slots = slot_ref[0]

    def do_round(r, carry):
        starts = [meta_ref[i * n_experts + e] + r * SLOT_WIN for e in range(n_experts)]
        onehot = jnp.concatenate(
            [_onehot_rows(slots[e:e + 1, :], starts[e], SLOT_WIN) for e in range(n_experts)], axis=0)
        contrib = jnp.dot(onehot.astype(F32).astype(BF16), xb, preferred_element_type=F32)
        for e in range(n_experts):
            data = contrib[e * SLOT_WIN:(e + 1) * SLOT_WIN, :].astype(BF16)
            head = jnp.where(r == 0, carry_ref[e], jnp.zeros_like(carry_ref[e]))

            @pl.when(jnp.logical_or(i > 0, r > 0))
            def _(e=e):
                window_copy(e, last_ref[e]).wait()

            stage_ref[e, :SLOT_ALIGN, :] = data[:SLOT_ALIGN, :] + head
            stage_ref[e, SLOT_ALIGN:, :] = data[SLOT_ALIGN:, :]
            window_copy(e, starts[e]).start()
            last_ref[e] = starts[e]
            rem = meta_ref[(n_tiles + i) * n_experts + e] - starts[e]
            k = jnp.clip(rem // SLOT_ALIGN, 0, SLOT_WIN // SLOT_ALIGN - 1)
            tail = stage_ref[e, pl.ds(pl.multiple_of(k * SLOT_ALIGN, SLOT_ALIGN), SLOT_ALIGN), :]
            is_last = jnp.logical_and(rem >= 0, rem < SLOT_WIN)
            carry_ref[e] = jnp.where(is_last, tail, carry_ref[e])
        return carry

    lax.fori_loop(0, meta_ref[3 * n_tiles * n_experts + i], do_round, 0)

    @pl.when(i == n_tiles - 1)
    def _():
        for e in range(n_experts):
            window_copy(e, last_ref[e]).wait()


def _dispatch(x2d, slots_t, meta, cap_total):
    n, d = x2d.shape
    n_tiles, ne, tt = slots_t.shape
    pad_windows = tt // SLOT_WIN + 1
    pad_rows = pad_windows * SLOT_WIN
    kern = functools.partial(_dispatch_kernel, n_tiles=n_tiles, n_experts=ne, cap_total=cap_total,
                             pad_windows=pad_windows)
    return pl.pallas_call(
        kern,
        grid_spec=pltpu.PrefetchScalarGridSpec(
            num_scalar_prefetch=1,
            grid=(n_tiles,),
            in_specs=[
                pl.BlockSpec((tt, d), lambda i, b: (i, 0)),
                pl.BlockSpec((1, ne, tt), lambda i, b: (i, 0, 0)),
            ],
            out_specs=pl.BlockSpec(memory_space=pl.ANY),
            scratch_shapes=[
                pltpu.VMEM((ne, SLOT_ALIGN, d), BF16),
                pltpu.VMEM((ne, SLOT_WIN, d), BF16),
                pltpu.SemaphoreType.DMA((ne,)),
                pltpu.SMEM((ne,), I32),
            ],
        ),
        out_shape=jax.ShapeDtypeStruct((ne, cap_total + pad_rows, d), BF16),
        compiler_params=_params(("arbitrary",)),
        name="moe_dispatch",
    )(meta, x2d, slots_t)


def _ffn_kernel(x_ref, wg_ref, wu_ref, wd_ref, o_ref, *, chunk):
    x = x_ref[0]
    ff = wg_ref.shape[2]
    acc = None
    for c in range(ff // chunk):
        cols = slice(c * chunk, (c + 1) * chunk)
        hg = jnp.dot(x, wg_ref[0, :, cols], preferred_element_type=F32)
        hu = jnp.dot(x, wu_ref[0, :, cols], preferred_element_type=F32)
        h = (hg * jax.nn.sigmoid(hg) * hu).astype(BF16)
        part = jnp.dot(h, wd_ref[0, cols, :], preferred_element_type=F32)
        acc = part if acc is None else acc + part
    o_ref[0] = acc.astype(BF16)


def _expert_ffn(xe, wg, wu, wd, cap_total, layer):
    ne, _, d = xe.shape
    ff = wg.shape[2]
    tm = FFN_ROWS
    while cap_total % tm:
        tm //= 2
    chunk = MXU_DIM if ff % MXU_DIM == 0 else LANES
    w0 = layer * ne
    return pl.pallas_call(
        functools.partial(_ffn_kernel, chunk=chunk),
        grid=(ne, cap_total // tm),
        in_specs=[
            pl.BlockSpec((1, tm, d), lambda e, i: (e, i, 0)),
            pl.BlockSpec((1, d, ff), lambda e, i: (w0 + e, 0, 0)),
            pl.BlockSpec((1, d, ff), lambda e, i: (w0 + e, 0, 0)),
            pl.BlockSpec((1, ff, d), lambda e, i: (w0 + e, 0, 0)),
        ],
        out_specs=pl.BlockSpec((1, tm, d), lambda e, i: (e, i, 0)),
        out_shape=jax.ShapeDtypeStruct((ne, cap_total, d), BF16),
        compiler_params=_params(("parallel", "arbitrary")),
        name="moe_expert_ffn",
    )(xe, wg, wu, wd)


def _combine_kernel(meta_ref, *refs, n_tiles, n_experts, cap_total, alpha, split_tile, n_cast, cast_steps):
    win_refs = refs[:n_experts]
    slot_ref, gate_ref, x_ref, g_ref, beta_ref, ye_ref = refs[n_experts:n_experts + 6]
    cast_in = refs[n_experts + 6:n_experts + 6 + n_cast]
    outs = refs[n_experts + 6 + n_cast:-2]
    out_refs, cast_out = outs[:len(outs) - n_cast], outs[len(outs) - n_cast:]
    extra_ref, sem_ref = refs[-2:]
    i = pl.program_id(0)

    if n_cast:
        @pl.when(i < cast_steps)
        def _():
            for src, dst in zip(cast_in, cast_out):
                dst[...] = src[...].astype(BF16)

    slots = slot_ref[0]
    gates = gate_ref[0]

    te = n_tiles * n_experts

    def weights(lowers, starts):
        pieces = []
        for e in range(n_experts):
            srow = slots[e:e + 1, :]
            hit = jnp.logical_and(_onehot_rows(srow, starts[e], SLOT_WIN), srow >= lowers[e])
            pieces.append(jnp.where(hit, gates[e:e + 1, :], 0.0))
        return jnp.concatenate(pieces, axis=0).T.astype(BF16)

    lowers0 = [meta_ref[i * n_experts + e] for e in range(n_experts)]
    starts0 = [meta_ref[2 * te + i * n_experts + e] for e in range(n_experts)]
    ye0 = jnp.concatenate([r[0] for r in win_refs], axis=0)
    y = jnp.dot(weights(lowers0, starts0), ye0, preferred_element_type=F32)

    def extra_round(r, y):
        lowers = [lowers0[e] + r * SLOT_WIN for e in range(n_experts)]
        starts = [jnp.minimum(lowers[e], cap_total - SLOT_WIN) for e in range(n_experts)]
        for e in range(n_experts):
            rows = pl.ds(pl.multiple_of(starts[e], SLOT_ALIGN), SLOT_WIN)
            cp = pltpu.make_async_copy(ye_ref.at[e, rows, :],
                                       extra_ref.at[pl.ds(e * SLOT_WIN, SLOT_WIN), :], sem_ref.at[0])
            cp.start()
            cp.wait()
        return y + jnp.dot(weights(lowers, starts), extra_ref[...], preferred_element_type=F32)

    y = lax.fori_loop(1, meta_ref[3 * te + n_tiles + i], extra_round, y)
    out = _layer_norm(alpha * x_ref[...] + y, g_ref[...], beta_ref[...])
    if split_tile is None:
        out_refs[0][...] = out
    else:
        @pl.when(i < split_tile)
        def _():
            out_refs[0][...] = out

        @pl.when(i >= split_tile)
        def _():
            out_refs[1][...] = out


def _cast_split(n_steps, n_lead, dims):
    split = 1
    while n_lead * split * 2 <= n_steps and all(r % (split * 2 * SLOT_ALIGN) == 0 for r in dims):
        split *= 2
    return split


def _combine_ln(ye, slots_t, gates_t, meta, x2d, g, beta, alpha, split_rows=None, cast=()):
    n, d = x2d.shape
    n_tiles, ne, tt = slots_t.shape
    cap_total = ye.shape[1]
    split_tile = None if split_rows is None else split_rows // tt
    split = _cast_split(n_tiles, ne, [w.shape[1] for w in cast]) if cast else 1
    cast_steps = ne * split
    assert not cast or cast_steps <= n_tiles
    kern = functools.partial(_combine_kernel, n_tiles=n_tiles, n_experts=ne, cap_total=cap_total, alpha=alpha,
                             split_tile=split_tile, n_cast=len(cast), cast_steps=cast_steps)

    def cast_spec(w):
        def imap(i, meta):
            s = jnp.minimum(i, cast_steps - 1)
            return (s // split, s % split, 0)
        return pl.BlockSpec((1, w.shape[1] // split, w.shape[2]), imap)

    def win_spec(e):
        def imap(i, meta):
            return (e, pl.multiple_of(meta[(2 * n_tiles + i) * ne + e], SLOT_ALIGN), 0)
        return pl.BlockSpec((pl.Element(1), pl.Element(SLOT_WIN), pl.Element(d)), imap)

    in_specs = [win_spec(e) for e in range(ne)]
    in_specs += [
        pl.BlockSpec((1, ne, tt), lambda i, b: (i, 0, 0)),
        pl.BlockSpec((1, ne, tt), lambda i, b: (i, 0, 0)),
        pl.BlockSpec((tt, d), lambda i, b: (i, 0)),
        pl.BlockSpec((1, d), lambda i, b: (0, 0)),
        pl.BlockSpec((1, d), lambda i, b: (0, 0)),
        pl.BlockSpec(memory_space=pl.ANY),
    ]
    in_specs += [cast_spec(w) for w in cast]
    if split_tile is None:
        out_specs = [pl.BlockSpec((tt, d), lambda i, b: (i, 0))]
        out_shape = [jax.ShapeDtypeStruct((n, d), F32)]
    else:
        out_specs = [pl.BlockSpec((tt, d), lambda i, b: (jnp.minimum(i, split_tile - 1), 0)),
                     pl.BlockSpec((tt, d), lambda i, b: (jnp.maximum(i - split_tile, 0), 0))]
        out_shape = [jax.ShapeDtypeStruct((split_rows, d), F32),
                     jax.ShapeDtypeStruct((n - split_rows, d), F32)]
    out_specs += [cast_spec(w) for w in cast]
    out_shape += [jax.ShapeDtypeStruct(w.shape, BF16) for w in cast]
    return pl.pallas_call(
        kern,
        grid_spec=pltpu.PrefetchScalarGridSpec(
            num_scalar_prefetch=1,
            grid=(n_tiles,),
            in_specs=in_specs,
            out_specs=out_specs,
            scratch_shapes=[
                pltpu.VMEM((ne * SLOT_WIN, d), BF16),
                pltpu.SemaphoreType.DMA((1,)),
            ],
        ),
        out_shape=out_shape,
        compiler_params=_params(("arbitrary",)),
        name="moe_combine_ln",
    )(meta, *([ye] * ne), slots_t, gates_t, x2d, g, beta, ye, *cast)


def _filter_kernel(w1t_ref, w1c_ref, w1s_ref, b1_ref, fq_ref, w2_ref, b2_ref, w3_ref, fr_ref, dl_ref,
                   o_ref, *, seq_len):
    half = pl.program_id(0)
    pos = lax.broadcasted_iota(I32, (1, seq_len), 1)
    lag = jnp.where(half == 0, seq_len - pos, pos)
    lagf = lag.astype(F32)
    t = lagf / (seq_len - 1.0)
    w = (2.0 * math.pi) * lagf / seq_len
    fw = fr_ref[...] * w
    hp = lax.Precision.HIGHEST
    pre = (w1t_ref[...] * t
           + jnp.dot(w1c_ref[...], jnp.cos(fw), precision=hp, preferred_element_type=F32)
           - jnp.dot(w1s_ref[...], jnp.sin(fw), precision=hp, preferred_element_type=F32)
           + b1_ref[...])
    fq = fq_ref[...]
    h = jnp.sin(fq * pre)
    h = jnp.sin(fq * (jnp.dot(w2_ref[...], h, precision=hp, preferred_element_type=F32) + b2_ref[...]))
    out = jnp.dot(w3_ref[0], h, precision=hp, preferred_element_type=F32)
    out = out * jnp.exp(-t * dl_ref[...])
    o_ref[...] = jnp.where(lag < seq_len, out, 0.0)


def _hyena_filter(f_w1, f_b1, f_freq, f_w2, f_b2, f_w3, seq_len, d):
    hid = f_w1.shape[1]
    bands = (FILTER_EMB - 1) // 2
    w1 = f_w1.astype(F32).T
    col = lambda v: v.astype(F32).reshape(-1, 1)
    freqs = jnp.linspace(1e-4, bands - 1, bands, dtype=F32).reshape(bands, 1)
    min_decay = math.log(DECAY_TARGET) / SLOW_DECAY_PCT
    max_decay = math.log(DECAY_TARGET) / FAST_DECAY_PCT
    deltas = jnp.abs(jnp.linspace(min_decay, max_decay, d, dtype=F32)).reshape(d, 1)
    w3 = f_w3.astype(F32).T.reshape(2, d, hid)
    full = lambda shape: pl.BlockSpec(shape, lambda s: (0,) * len(shape))
    return pl.pallas_call(
        functools.partial(_filter_kernel, seq_len=seq_len),
        grid=(2,),
        in_specs=[
            full((hid, 1)), full((hid, bands)), full((hid, bands)), full((hid, 1)), full((hid, 1)),
            full((hid, hid)), full((hid, 1)),
            pl.BlockSpec((1, d, hid), lambda s: (1 - s, 0, 0)),
            full((bands, 1)), full((d, 1)),
        ],
        out_specs=pl.BlockSpec((d, seq_len), lambda s: (0, s)),
        out_shape=jax.ShapeDtypeStruct((d, 2 * seq_len), F32),
        compiler_params=_params(("arbitrary",)),
        name="hyena_filter",
    )(w1[:, :1], w1[:, 1:1 + bands], w1[:, 1 + bands:], col(f_b1), col(f_freq),
      f_w2.astype(F32).T, col(f_b2), w3, freqs, deltas)


def _hyena_in_kernel(x_ref, w0_ref, w1_ref, w2_ref, b_ref, cw_ref, cb_ref, x0_ref, z_ref, xb_ref):
    c = pl.program_id(1)

    @pl.when(c == 0)
    def _():
        xb_ref[...] = x_ref[0].astype(BF16)

    xb = xb_ref[...]
    seq_len = xb.shape[0]
    row = lax.broadcasted_iota(I32, (seq_len, 1), 0)
    not_first = row > 0
    not_last = row < seq_len - 1

    def branch(w_ref, k):
        u = jnp.dot(xb, w_ref[...], preferred_element_type=F32) + b_ref[k]
        prev = jnp.where(not_first, pltpu.roll(u, 1, axis=0), 0.0)
        nxt = jnp.where(not_last, pltpu.roll(u, seq_len - 1, axis=0), 0.0)
        cw = cw_ref[k]
        return prev * cw[0:1] + u * cw[1:2] + nxt * cw[2:3] + cb_ref[k]

    x0_ref[0] = branch(w0_ref, 0).T
    z_ref[0] = (branch(w2_ref, 2) * branch(w1_ref, 1)).T


def _hyena_in(x3d, w_in, b_in, conv_w, conv_b):
    b, l, d = x3d.shape
    ch = HY_CH
    nc = d // ch
    b3 = b_in.astype(F32).reshape(3, 1, d)
    cw3 = conv_w.astype(F32).reshape(conv_w.shape[0], 3, d).transpose(1, 0, 2)
    cb3 = conv_b.astype(F32).reshape(3, 1, d)
    wspec = lambda k: pl.BlockSpec((d, ch), lambda i, c, k=k: (0, k * nc + c))
    return pl.pallas_call(
        _hyena_in_kernel,
        grid=(b, nc),
        in_specs=[
            pl.BlockSpec((1, l, d), lambda i, c: (i, 0, 0)),
            wspec(0), wspec(1), wspec(2),
            pl.BlockSpec((3, 1, ch), lambda i, c: (0, 0, c)),
            pl.BlockSpec((3, 3, ch), lambda i, c: (0, 0, c)),
            pl.BlockSpec((3, 1, ch), lambda i, c: (0, 0, c)),
        ],
        out_specs=[
            pl.BlockSpec((1, ch, l), lambda i, c: (i, c, 0)),
            pl.BlockSpec((1, ch, l), lambda i, c: (i, c, 0)),
        ],
        out_shape=[
            jax.ShapeDtypeStruct((b, d, l), F32),
            jax.ShapeDtypeStruct((b, d, l), F32),
        ],
        scratch_shapes=[pltpu.VMEM((l, d), BF16)],
        compiler_params=_params(("parallel", "arbitrary")),
        name="hyena_in_conv",
    )(x3d, w_in, w_in, w_in, b3, cw3, cb3)


def _long_conv_kernel(z_ref, x0_ref, kk_ref, skip_ref, o_ref, *, n_blk):
    blk = CONV_BLOCK
    nb = z_ref.shape[0]

    for c in range(z_ref.shape[1]):
        zc = z_ref[:, c, :]
        zr = jnp.concatenate([zc[:, j * blk:(j + 1) * blk] for j in range(n_blk)], axis=0).astype(BF16)
        kk = kk_ref[c:c + 1, :]
        acc = [None] * n_blk
        for delta in range(-(n_blk - 1), n_blk):
            start = (n_blk + delta - 1) * blk
            wrow = jnp.broadcast_to(kk[:, start:start + 2 * blk], (blk, 2 * blk))
            rolled = pltpu.roll(wrow, 0, axis=1, stride=1, stride_axis=0)
            tt = rolled[:, blk:].astype(BF16)
            j0, j1 = max(0, -delta), min(n_blk, n_blk - delta)
            res = jnp.dot(zr[j0 * nb:j1 * nb, :], tt, preferred_element_type=F32)
            for j in range(j0, j1):
                part = res[(j - j0) * nb:(j - j0 + 1) * nb, :]
                i = j + delta
                acc[i] = part if acc[i] is None else acc[i] + part
        y = jnp.concatenate(acc, axis=1)
        o_ref[:, c, :] = (y + zc * skip_ref[c:c + 1, :]) * x0_ref[:, c, :]


def _long_conv(z_t, x0_t, kk, skip):
    b, d, l = z_t.shape
    ch = CONV_CH
    kern = functools.partial(_long_conv_kernel, n_blk=l // CONV_BLOCK)
    return pl.pallas_call(
        kern,
        grid=(d // ch,),
        in_specs=[
            pl.BlockSpec((b, ch, l), lambda i: (0, i, 0)),
            pl.BlockSpec((b, ch, l), lambda i: (0, i, 0)),
            pl.BlockSpec((ch, 2 * l), lambda i: (i, 0)),
            pl.BlockSpec((ch, 1), lambda i: (i, 0)),
        ],
        out_specs=pl.BlockSpec((b, ch, l), lambda i: (0, i, 0)),
        out_shape=jax.ShapeDtypeStruct((b, d, l), F32),
        compiler_params=_params(("parallel",)),
        name="hyena_long_conv",
    )(z_t, x0_t, kk, skip)


def _rope_tables(seq_len, head_dim):
    rows = seq_len // GRID_W
    row = jnp.repeat(jnp.arange(rows), GRID_W)
    col = jnp.tile(jnp.arange(GRID_W), rows)
    axis_dim = head_dim // 2
    inv = ROPE_THETA ** (-jnp.arange(0, axis_dim, 2, dtype=F32) / axis_dim)
    ang = jnp.concatenate([row[:, None] * inv, col[:, None] * inv], -1)
    cos, sin = jnp.cos(ang), jnp.sin(ang)
    reps = LANES // head_dim
    cc = jnp.tile(jnp.concatenate([cos, cos], -1), (1, reps))
    ss = jnp.tile(jnp.concatenate([-sin, sin], -1), (1, reps))
    return cc, ss


def _moe(x1, aff, group_tokens, w_gate, w_up, w_down, layer, g, beta, alpha, split_rows=None, cast=()):
    n, d = x1.shape
    ne = aff.shape[1]
    tt = TOK_TILE
    slots, gates, bases = [], [], []
    tok0, slot0 = 0, 0
    for ng in group_tokens:
        cap = EC_CAPACITY * ng // ne
        s, gt, bs = _select(aff[tok0:tok0 + ng].T.reshape(ne, ng // tt, tt), cap, slot0)
        slots.append(s)
        gates.append(gt)
        bases.append(bs[:, :, 0])
        tok0 += ng
        slot0 += cap
    cap_total = slot0
    slots_t = jnp.concatenate(slots, axis=1).transpose(1, 0, 2)
    gates_t = jnp.concatenate(gates, axis=1).transpose(1, 0, 2)
    base = jnp.concatenate(bases + [jnp.full((ne, 1), cap_total, I32)], axis=1).T
    meta = _routing_meta(base, cap_total)
    xe = _dispatch(x1, slots_t, meta, cap_total)
    ye = _expert_ffn(xe, w_gate, w_up, w_down, cap_total, layer)
    return _combine_ln(ye, slots_t, gates_t, meta, x1, g, beta, alpha, split_rows, cast)


def kernel(x_prompt, x_sample, attn_w_qkv, attn_q_gain, attn_k_gain, attn_w_o, hy_w_in, hy_b_in, hy_conv_w, hy_conv_b, hy_f_w1, hy_f_b1, hy_f_freq, hy_f_w2, hy_f_b2, hy_f_w3, hy_skip, hy_w_out, hy_b_out, ln_mix_g, ln_mix_b, moe_router, moe_w_gate, moe_w_up, moe_w_down, ln_ffn_g, ln_ffn_b):
    bp, l, d = x_prompt.shape
    bs = x_sample.shape[0]
    assert x_sample.shape[1] == l
    depth = ln_mix_g.shape[0]
    alpha = (2 * depth) ** 0.25
    head_dim = d // N_HEADS
    b = bp + bs
    n = b * l
    group_tokens = (bp * l, bs * l)

    xs = [x_prompt.reshape(bp * l, d), x_sample.reshape(bs * l, d)]
    cc, ss = _rope_tables(l, head_dim)
    seg = (lax.broadcasted_iota(I32, (MXU_DIM, MXU_DIM), 0) // head_dim
           == lax.broadcasted_iota(I32, (MXU_DIM, MXU_DIM), 1) // head_dim).astype(BF16)
    row = lambda v: v.astype(F32).reshape(1, -1)
    zero_bias = jnp.zeros((1, d), F32)
    moe_w = (moe_w_gate, moe_w_up, moe_w_down)
    w_bf16 = tuple(w[0].astype(BF16) for w in moe_w)
    cast_in_kernel = l * b // TOK_TILE >= moe_w_gate.shape[1]

    for i in range(depth):
        j = i // 2
        r_hi, r_lo = _split_bf16(moe_router[i].astype(F32))
        if i % 2 == 0:
            gq = jnp.tile(row(attn_q_gain[j]), (1, LANES // head_dim))
            gk = jnp.tile(row(attn_k_gain[j]), (1, LANES // head_dim))
            q, k, v = _qkv_rope(xs, attn_w_qkv[j].astype(BF16), cc, ss, gq, gk, seg, l)
            o = _attention(q.reshape(b, l, -1), k.reshape(b, l, -1), v.reshape(b, l, -1), tq=min(ATTN_STEP, l))
            mix_in, w_mix, b_mix = o.reshape(n, -1), attn_w_o[j].astype(BF16), zero_bias
        else:
            kk = _hyena_filter(hy_f_w1[j], hy_f_b1[j], hy_f_freq[j], hy_f_w2[j], hy_f_b2[j], hy_f_w3[j], l, d)
            (x,) = xs
            x0, z = _hyena_in(x.reshape(b, l, d), hy_w_in[j].astype(BF16), hy_b_in[j], hy_conv_w[j], hy_conv_b[j])
            mix_in = _long_conv(z, x0, kk, hy_skip[j].astype(F32).reshape(d, 1))
            w_mix, b_mix = hy_w_out[j].astype(BF16), row(hy_b_out[j])
        x1, aff = _proj_ln_router(mix_in, w_mix, b_mix, xs, row(ln_mix_g[i]), row(ln_mix_b[i]),
                                  r_hi, r_lo, alpha)
        last = i == depth - 1
        cast = tuple(w[i + 1] for w in moe_w) if cast_in_kernel and not last else ()
        out = _moe(x1, aff, group_tokens, *w_bf16, 0, row(ln_ffn_g[i]), row(ln_ffn_b[i]), alpha,
                   split_rows=group_tokens[0] if last else None, cast=cast)
        if last:
            return (out[0].reshape(bp, l, d), out[1].reshape(bs, l, d))
        xs = [out[0]]
        w_bf16 = tuple(out[1:]) if cast else tuple(w[i + 1].astype(BF16) for w in moe_w)
```

```python
import functools
import math

import jax
import jax.numpy as jnp
from jax import lax
from jax.experimental import pallas as pl
from jax.experimental.pallas import tpu as pltpu

F32 = jnp.float32
BF16 = jnp.bfloat16
I32 = jnp.int32

N_HEADS = 16
N_KV_HEADS = 4
GRID_W = 64
ROPE_THETA = 10000.0
QK_EPS = 1e-6
FILTER_EMB = 33
DECAY_TARGET = 1e-2
FAST_DECAY_PCT = 0.3
SLOW_DECAY_PCT = 1.5
EC_CAPACITY = 2
LN_EPS = 1e-5

LANES = 128
MXU_DIM = 256
VMEM_LIMIT_BYTES = 56 * 1024 * 1024

ROW_TILE = 512
PROJ_ROWS = 256
ATTN_ROWS = 256
ATTN_STEP = 512
TOK_TILE = 256
SELECT_EXPERTS = 4
SLOT_ALIGN = 16
SLOT_WIN = 64
FFN_ROWS = 1024
CONV_BLOCK = 256
CONV_CH = 8
HY_CH = 256


def _params(sem, vmem=VMEM_LIMIT_BYTES):
    return pltpu.CompilerParams(dimension_semantics=sem, vmem_limit_bytes=vmem)


def _layer_norm(v, g, b):
    mu = jnp.mean(v, axis=-1, keepdims=True)
    c = v - mu
    var = jnp.mean(c * c, axis=-1, keepdims=True)
    return c * lax.rsqrt(var + LN_EPS) * g + b


def _split_bf16(v):
    hi = v.astype(BF16)
    lo = (v - hi.astype(F32)).astype(BF16)
    return hi, lo


def _token_specs(xs, tm):
    d = xs[0].shape[1]
    if len(xs) == 1:
        return [pl.BlockSpec((tm, d), lambda i: (i, 0))], None
    first = xs[0].shape[0] // tm
    return [pl.BlockSpec((tm, d), lambda i: (jnp.minimum(i, first - 1), 0)),
            pl.BlockSpec((tm, d), lambda i: (jnp.maximum(i - first, 0), 0))], first


def _token_tile(x_refs, first_tiles):
    if first_tiles is None:
        return x_refs[0][...]
    return jnp.where(pl.program_id(0) < first_tiles, x_refs[0][...], x_refs[1][...])


def _qkv_kernel(*refs, n_x, first_tiles, n_q_chunks, n_k_chunks, head_dim, scale):
    w_ref, cc_ref, ss_ref, gq_ref, gk_ref, seg_ref, q_ref, k_ref, v_ref = refs[n_x:]
    xb = _token_tile(refs[:n_x], first_tiles).astype(BF16)
    qkv = jnp.dot(xb, w_ref[...], preferred_element_type=F32)
    cc = cc_ref[...]
    ss = ss_ref[...]
    seg = seg_ref[...]
    lane = lax.broadcasted_iota(I32, (xb.shape[0], LANES), 1)
    first_half = (lane % head_dim) < (head_dim // 2)
    half = head_dim // 2
    per = MXU_DIM // LANES
    rinv = []
    for c0 in range(0, n_q_chunks + n_k_chunks, per):
        u2 = qkv[:, c0 * LANES:(c0 + per) * LANES]
        s_hi, s_lo = _split_bf16(u2 * u2)
        ssum = (jnp.dot(s_hi, seg, preferred_element_type=F32)
                + jnp.dot(s_lo, seg, preferred_element_type=F32))
        r2 = lax.rsqrt(ssum * (1.0 / head_dim) + QK_EPS)
        rinv += [r2[:, h * LANES:(h + 1) * LANES] for h in range(per)]
    for c in range(n_q_chunks + n_k_chunks):
        u = qkv[:, c * LANES:(c + 1) * LANES]
        r = rinv[c]
        gain = gq_ref[...] if c < n_q_chunks else gk_ref[...]
        un = u * r * gain
        partner = jnp.where(first_half, pltpu.roll(un, LANES - half, axis=1), pltpu.roll(un, half, axis=1))
        o = un * cc + partner * ss
        if c < n_q_chunks:
            q_ref[:, c * LANES:(c + 1) * LANES] = (o * scale).astype(BF16)
        else:
            ck = c - n_q_chunks
            k_ref[:, ck * LANES:(ck + 1) * LANES] = o.astype(BF16)
    v_ref[...] = qkv[:, (n_q_chunks + n_k_chunks) * LANES:].astype(BF16)


def _qkv_rope(xs, w_qkv, cc, ss, gq, gk, seg, seq_len):
    n = sum(x.shape[0] for x in xs)
    d = xs[0].shape[1]
    head_dim = d // N_HEADS
    dq = N_HEADS * head_dim
    dk = N_KV_HEADS * head_dim
    tm = ROW_TILE
    tiles_per_seq = seq_len // tm
    x_specs, first_tiles = _token_specs(xs, tm)
    kern = functools.partial(_qkv_kernel, n_x=len(xs), first_tiles=first_tiles,
                             n_q_chunks=dq // LANES, n_k_chunks=dk // LANES,
                             head_dim=head_dim, scale=head_dim ** -0.5 * math.log2(math.e))
    return pl.pallas_call(
        kern,
        grid=(n // tm,),
        in_specs=x_specs + [
            pl.BlockSpec((d, dq + 2 * dk), lambda i: (0, 0)),
            pl.BlockSpec((tm, LANES), lambda i: (i % tiles_per_seq, 0)),
            pl.BlockSpec((tm, LANES), lambda i: (i % tiles_per_seq, 0)),
            pl.BlockSpec((1, LANES), lambda i: (0, 0)),
            pl.BlockSpec((1, LANES), lambda i: (0, 0)),
            pl.BlockSpec((MXU_DIM, MXU_DIM), lambda i: (0, 0)),
        ],
        out_specs=[
            pl.BlockSpec((tm, dq), lambda i: (i, 0)),
            pl.BlockSpec((tm, dk), lambda i: (i, 0)),
            pl.BlockSpec((tm, dk), lambda i: (i, 0)),
        ],
        out_shape=[
            jax.ShapeDtypeStruct((n, dq), BF16),
            jax.ShapeDtypeStruct((n, dk), BF16),
            jax.ShapeDtypeStruct((n, dk), BF16),
        ],
        compiler_params=_params(("parallel",)),
        name="attn_qkv_rope",
    )(*xs, w_qkv, cc, ss, gq, gk, seg)


def _attn_kernel(q_ref, k_ref, v_ref, o_ref, *, head_dim, group):
    tq = q_ref.shape[1]
    n_keys = k_ref.shape[1]
    lane = lax.broadcasted_iota(I32, (n_keys, head_dim), 1)
    ones_col = jnp.where(lane == 0, 1.0, 0.0).astype(BF16)
    for kv in range(N_KV_HEADS):
        k = k_ref[0, :, kv * head_dim:(kv + 1) * head_dim]
        v = v_ref[0, :, kv * head_dim:(kv + 1) * head_dim]
        v1 = jnp.concatenate([v, ones_col], axis=1)
        for r0 in range(0, tq, ATTN_ROWS):
            rows = slice(r0, r0 + ATTN_ROWS)
            qs = [q_ref[0, rows, (kv * group + g) * head_dim:(kv * group + g + 1) * head_dim]
                  for g in range(group)]
            q = jnp.concatenate(qs, axis=0)
            s = lax.dot_general(q, k, (((1,), (1,)), ((), ())), preferred_element_type=F32)
            m = jnp.max(s, axis=-1, keepdims=True)
            p = jnp.exp2(s - m).astype(BF16)
            ov = jnp.dot(p, v1, preferred_element_type=F32)
            o = ov[:, :head_dim] / ov[:, head_dim:head_dim + 1]
            outs = [o[g * ATTN_ROWS:(g + 1) * ATTN_ROWS, :] for g in range(group)]
            o_ref[0, rows, kv * group * head_dim:(kv + 1) * group * head_dim] = (
                jnp.concatenate(outs, axis=1).astype(BF16))


def _attention(q, k, v, tq):
    b, l, dq = q.shape
    dk = k.shape[2]
    head_dim = dq // N_HEADS
    kern = functools.partial(_attn_kernel, head_dim=head_dim, group=N_HEADS // N_KV_HEADS)
    return pl.pallas_call(
        kern,
        grid=(b, l // tq),
        in_specs=[
            pl.BlockSpec((1, tq, dq), lambda i, j: (i, j, 0)),
            pl.BlockSpec((1, l, dk), lambda i, j: (i, 0, 0)),
            pl.BlockSpec((1, l, dk), lambda i, j: (i, 0, 0)),
        ],
        out_specs=pl.BlockSpec((1, tq, dq), lambda i, j: (i, j, 0)),
        out_shape=jax.ShapeDtypeStruct((b, l, dq), BF16),
        compiler_params=_params(("parallel", "parallel")),
        name="attn_core",
    )(q, k, v)


def _proj_ln_router_kernel(*refs, n_x, first_tiles, alpha, channel_major):
    m_ref, w_ref, b_ref, g_ref, beta_ref, rc_ref, rh_ref, y_ref, aff_ref = refs[n_x:]
    m = m_ref[0].T.astype(BF16) if channel_major else m_ref[...]
    mix = jnp.dot(m, w_ref[...], preferred_element_type=F32) + b_ref[...]
    y = _layer_norm(alpha * _token_tile(refs[:n_x], first_tiles) + mix, g_ref[...], beta_ref[...])
    y_ref[...] = y
    y_hi, y_lo = _split_bf16(y)
    ne = rh_ref.shape[1]
    both = jnp.dot(y_hi, rc_ref[...], preferred_element_type=F32)
    lg = both[:, :ne] + both[:, ne:] + jnp.dot(y_lo, rh_ref[...], preferred_element_type=F32)
    ex = jnp.exp(lg - jnp.max(lg, axis=-1, keepdims=True))
    aff_ref[...] = ex / jnp.sum(ex, axis=-1, keepdims=True)


def _proj_ln_router(mix_in, w, bias, xs, g, beta, r_hi, r_lo, alpha):
    n = sum(x.shape[0] for x in xs)
    d = xs[0].shape[1]
    ne = r_hi.shape[1]
    tm = PROJ_ROWS
    channel_major = mix_in.ndim == 3
    if channel_major:
        dm = mix_in.shape[1]
        tiles_per_seq = mix_in.shape[2] // tm
        mix_spec = pl.BlockSpec((1, dm, tm), lambda i: (i // tiles_per_seq, 0, i % tiles_per_seq))
    else:
        dm = mix_in.shape[1]
        mix_spec = pl.BlockSpec((tm, dm), lambda i: (i, 0))
    x_specs, first_tiles = _token_specs(xs, tm)
    kern = functools.partial(_proj_ln_router_kernel, n_x=len(xs), first_tiles=first_tiles, alpha=alpha,
                             channel_major=channel_major)
    return pl.pallas_call(
        kern,
        grid=(n // tm,),
        in_specs=x_specs + [
            mix_spec,
            pl.BlockSpec((dm, d), lambda i: (0, 0)),
            pl.BlockSpec((1, d), lambda i: (0, 0)),
            pl.BlockSpec((1, d), lambda i: (0, 0)),
            pl.BlockSpec((1, d), lambda i: (0, 0)),
            pl.BlockSpec((d, 2 * ne), lambda i: (0, 0)),
            pl.BlockSpec((d, ne), lambda i: (0, 0)),
        ],
        out_specs=[
            pl.BlockSpec((tm, d), lambda i: (i, 0)),
            pl.BlockSpec((tm, ne), lambda i: (i, 0)),
        ],
        out_shape=[
            jax.ShapeDtypeStruct((n, d), F32),
            jax.ShapeDtypeStruct((n, ne), F32),
        ],
        compiler_params=_params(("parallel",)),
        name="proj_ln_router",
    )(*xs, mix_in, w, bias, g, beta, jnp.concatenate([r_hi, r_lo], axis=1), r_hi)


def _select_kernel(aff_ref, tri_ref, low_ref, slot_ref, gate_ref, base_ref, *, cap, slot_offset):
    n_exp, nt, tt = aff_ref.shape
    affs = [aff_ref[j] for j in range(n_exp)]
    bits = [pltpu.bitcast(a, I32) for a in affs]

    def count(mask):
        c = jnp.sum(mask.astype(F32), axis=1, keepdims=True)
        return jnp.sum(c, axis=0, keepdims=True)

    def step(i, prefixes):
        bit = jnp.left_shift(jnp.int32(1), 30 - i)
        cands = [p | bit for p in prefixes]
        return tuple(jnp.where(count(b >= c) >= cap, c, p) for b, c, p in zip(bits, cands, prefixes))

    thrs = lax.fori_loop(0, 31, step, tuple(jnp.zeros((1, 1), I32) for _ in range(n_exp)))

    tri = tri_ref[...]
    low = low_ref[...]
    ones = jnp.ones((tt, LANES), BF16)

    def excl_cumsum(mask):
        mf = mask.astype(F32).astype(BF16)
        within = jnp.dot(mf, tri, preferred_element_type=F32)
        tot = jnp.dot(mf, ones, preferred_element_type=F32).astype(BF16)
        base = jnp.dot(low, tot, preferred_element_type=F32)
        return within + base[:, :1], base

    for j in range(n_exp):
        gt = bits[j] > thrs[j]
        eq = bits[j] == thrs[j]
        need = cap - count(gt)
        rank_eq, _ = excl_cumsum(eq)
        sel = gt | (eq & (rank_eq < need))
        pos, base = excl_cumsum(sel)
        slot_ref[j] = jnp.where(sel, pos.astype(I32) + slot_offset, -1)
        gate_ref[j] = jnp.where(sel, affs[j], 0.0)
        base_ref[j] = base.astype(I32) + slot_offset


def _select(aff_t, cap, slot_offset):
    ne, nt, tt = aff_t.shape
    tri = (lax.broadcasted_iota(I32, (tt, tt), 0) < lax.broadcasted_iota(I32, (tt, tt), 1)).astype(BF16)
    low = (lax.broadcasted_iota(I32, (nt, nt), 0) > lax.broadcasted_iota(I32, (nt, nt), 1)).astype(BF16)
    kern = functools.partial(_select_kernel, cap=cap, slot_offset=slot_offset)
    eb = math.gcd(ne, SELECT_EXPERTS)
    return pl.pallas_call(
        kern,
        grid=(ne // eb,),
        in_specs=[
            pl.BlockSpec((eb, nt, tt), lambda e: (e, 0, 0)),
            pl.BlockSpec((tt, tt), lambda e: (0, 0)),
            pl.BlockSpec((nt, nt), lambda e: (0, 0)),
        ],
        out_specs=[
            pl.BlockSpec((eb, nt, tt), lambda e: (e, 0, 0)),
            pl.BlockSpec((eb, nt, tt), lambda e: (e, 0, 0)),
            pl.BlockSpec((eb, nt, LANES), lambda e: (e, 0, 0)),
        ],
        out_shape=[
            jax.ShapeDtypeStruct((ne, nt, tt), I32),
            jax.ShapeDtypeStruct((ne, nt, tt), F32),
            jax.ShapeDtypeStruct((ne, nt, LANES), I32),
        ],
        compiler_params=_params(("parallel",)),
        name="moe_select",
    )(aff_t, tri, low)


def _onehot_rows(slot_row, first_slot, n_rows):
    tt = slot_row.shape[1]
    rows = lax.broadcasted_iota(I32, (n_rows, tt), 0) + first_slot
    return rows == slot_row


def _routing_meta(base, cap_total):
    b0, b1 = base[:-1], base[1:]
    lower = (b0 // SLOT_ALIGN) * SLOT_ALIGN
    span = b1 - lower
    clamped = jnp.minimum(lower, cap_total - SLOT_WIN)
    rounds_dispatch = jnp.max(span // SLOT_WIN + 1, axis=1)
    rounds_combine = jnp.maximum(jnp.max((span + SLOT_WIN - 1) // SLOT_WIN, axis=1), 1)
    parts = [lower, b1, clamped, rounds_dispatch, rounds_combine]
    return jnp.concatenate([p.reshape(-1).astype(I32) for p in parts])


def _dispatch_kernel(meta_ref, x_ref, slot_ref, xe_ref, carry_ref, stage_ref, sem_ref, last_ref, *,
                     n_tiles, n_experts, cap_total, pad_windows):
    i = pl.program_id(0)

    def window_copy(e, start):
        rows = pl.ds(pl.multiple_of(start, SLOT_ALIGN), SLOT_WIN)
        return pltpu.make_async_copy(stage_ref.at[e], xe_ref.at[e, rows, :], sem_ref.at[e])

    @pl.when(i == 0)
    def _():
        carry_ref[...] = jnp.zeros_like(carry_ref)
        stage_ref[...] = jnp.zeros_like(stage_ref)
        pads = [window_copy(e, cap_total + k * SLOT_WIN) for e in range(n_experts) for k in range(pad_windows)]
        for cp in pads:
            cp.start()
        for cp in pads:
            cp.wait()

    xb = x_ref[...].astype(BF16)
    slots = slot_ref[0]

    def do_round(r, carry):
        starts = [meta_ref[i * n_experts + e] + r * SLOT_WIN for e in range(n_experts)]
        onehot = jnp.concatenate(
            [_onehot_rows(slots[e:e + 1, :], starts[e], SLOT_WIN) for e in range(n_experts)], axis=0)
        contrib = jnp.dot(onehot.astype(F32).astype(BF16), xb, preferred_element_type=F32)
        for e in range(n_experts):
            data = contrib[e * SLOT_WIN:(e + 1) * SLOT_WIN, :].astype(BF16)
            head = jnp.where(r == 0, carry_ref[e], jnp.zeros_like(carry_ref[e]))

            @pl.when(jnp.logical_or(i > 0, r > 0))
            def _(e=e):
                window_copy(e, last_ref[e]).wait()

            stage_ref[e, :SLOT_ALIGN, :] = data[:SLOT_ALIGN, :] + head
            stage_ref[e, SLOT_ALIGN:, :] = data[SLOT_ALIGN:, :]
            window_copy(e, starts[e]).start()
            last_ref[e] = starts[e]
            rem = meta_ref[(n_tiles + i) * n_experts + e] - starts[e]
            k = jnp.clip(rem // SLOT_ALIGN, 0, SLOT_WIN // SLOT_ALIGN - 1)
            tail = stage_ref[e, pl.ds(pl.multiple_of(k * SLOT_ALIGN, SLOT_ALIGN), SLOT_ALIGN), :]
            is_last = jnp.logical_and(rem >= 0, rem < SLOT_WIN)
            carry_ref[e] = jnp.where(is_last, tail, carry_ref[e])
        return carry

    lax.fori_loop(0, meta_ref[3 * n_tiles * n_experts + i], do_round, 0)

    @pl.when(i == n_tiles - 1)
    def _():
        for e in range(n_experts):
            window_copy(e, last_ref[e]).wait()


def _dispatch(x2d, slots_t, meta, cap_total):
    n, d = x2d.shape
    n_tiles, ne, tt = slots_t.shape
    pad_windows = tt // SLOT_WIN + 1
    pad_rows = pad_windows * SLOT_WIN
    kern = functools.partial(_dispatch_kernel, n_tiles=n_tiles, n_experts=ne, cap_total=cap_total,
                             pad_windows=pad_windows)
    return pl.pallas_call(
        kern,
        grid_spec=pltpu.PrefetchScalarGridSpec(
            num_scalar_prefetch=1,
            grid=(n_tiles,),
            in_specs=[
                pl.BlockSpec((tt, d), lambda i, b: (i, 0)),
                pl.BlockSpec((1, ne, tt), lambda i, b: (i, 0, 0)),
            ],
            out_specs=pl.BlockSpec(memory_space=pl.ANY),
            scratch_shapes=[
                pltpu.VMEM((ne, SLOT_ALIGN, d), BF16),
                pltpu.VMEM((ne, SLOT_WIN, d), BF16),
                pltpu.SemaphoreType.DMA((ne,)),
                pltpu.SMEM((ne,), I32),
            ],
        ),
        out_shape=jax.ShapeDtypeStruct((ne, cap_total + pad_rows, d), BF16),
        compiler_params=_params(("arbitrary",)),
        name="moe_dispatch",
    )(meta, x2d, slots_t)


def _ffn_kernel(x_ref, wg_ref, wu_ref, wd_ref, o_ref, *, chunk):
    x = x_ref[0]
    ff = wg_ref.shape[2]
    acc = None
    for c in range(ff // chunk):
        cols = slice(c * chunk, (c + 1) * chunk)
        hg = jnp.dot(x, wg_ref[0, :, cols], preferred_element_type=F32)
        hu = jnp.dot(x, wu_ref[0, :, cols], preferred_element_type=F32)
        h = (hg * jax.nn.sigmoid(hg) * hu).astype(BF16)
        part = jnp.dot(h, wd_ref[0, cols, :], preferred_element_type=F32)
        acc = part if acc is None else acc + part
    o_ref[0] = acc.astype(BF16)


def _expert_ffn(xe, wg, wu, wd, cap_total, layer):
    ne, _, d = xe.shape
    ff = wg.shape[2]
    tm = FFN_ROWS
    while cap_total % tm:
        tm //= 2
    chunk = MXU_DIM if ff % MXU_DIM == 0 else LANES
    w0 = layer * ne
    return pl.pallas_call(
        functools.partial(_ffn_kernel, chunk=chunk),
        grid=(ne, cap_total // tm),
        in_specs=[
            pl.BlockSpec((1, tm, d), lambda e, i: (e, i, 0)),
            pl.BlockSpec((1, d, ff), lambda e, i: (w0 + e, 0, 0)),
            pl.BlockSpec((1, d, ff), lambda e, i: (w0 + e, 0, 0)),
            pl.BlockSpec((1, ff, d), lambda e, i: (w0 + e, 0, 0)),
        ],
        out_specs=pl.BlockSpec((1, tm, d), lambda e, i: (e, i, 0)),
        out_shape=jax.ShapeDtypeStruct((ne, cap_total, d), BF16),
        compiler_params=_params(("parallel", "arbitrary")),
        name="moe_expert_ffn",
    )(xe, wg, wu, wd)


def _combine_kernel(meta_ref, *refs, n_tiles, n_experts, cap_total, alpha, split_tile, n_cast, cast_steps):
    win_refs = refs[:n_experts]
    slot_ref, gate_ref, x_ref, g_ref, beta_ref, ye_ref = refs[n_experts:n_experts + 6]
    cast_in = refs[n_experts + 6:n_experts + 6 + n_cast]
    outs = refs[n_experts + 6 + n_cast:-2]
    out_refs, cast_out = outs[:len(outs) - n_cast], outs[len(outs) - n_cast:]
    extra_ref, sem_ref = refs[-2:]
    i = pl.program_id(0)

    if n_cast:
        @pl.when(i < cast_steps)
        def _():
            for src, dst in zip(cast_in, cast_out):
                dst[...] = src[...].astype(BF16)

    slots = slot_ref[0]
    gates = gate_ref[0]

    te = n_tiles * n_experts

    def weights(lowers, starts):
        pieces = []
        for e in range(n_experts):
            srow = slots[e:e + 1, :]
            hit = jnp.logical_and(_onehot_rows(srow, starts[e], SLOT_WIN), srow >= lowers[e])
            pieces.append(jnp.where(hit, gates[e:e + 1, :], 0.0))
        return jnp.concatenate(pieces, axis=0).T.astype(BF16)

    lowers0 = [meta_ref[i * n_experts + e] for e in range(n_experts)]
    starts0 = [meta_ref[2 * te + i * n_experts + e] for e in range(n_experts)]
    ye0 = jnp.concatenate([r[0] for r in win_refs], axis=0)
    y = jnp.dot(weights(lowers0, starts0), ye0, preferred_element_type=F32)

    def extra_round(r, y):
        lowers = [lowers0[e] + r * SLOT_WIN for e in range(n_experts)]
        starts = [jnp.minimum(lowers[e], cap_total - SLOT_WIN) for e in range(n_experts)]
        for e in range(n_experts):
            rows = pl.ds(pl.multiple_of(starts[e], SLOT_ALIGN), SLOT_WIN)
            cp = pltpu.make_async_copy(ye_ref.at[e, rows, :],
                                       extra_ref.at[pl.ds(e * SLOT_WIN, SLOT_WIN), :], sem_ref.at[0])
            cp.start()
            cp.wait()
        return y + jnp.dot(weights(lowers, starts), extra_ref[...], preferred_element_type=F32)

    y = lax.fori_loop(1, meta_ref[3 * te + n_tiles + i], extra_round, y)
    out = _layer_norm(alpha * x_ref[...] + y, g_ref[...], beta_ref[...])
    if split_tile is None:
        out_refs[0][...] = out
    else:
        @pl.when(i < split_tile)
        def _():
            out_refs[0][...] = out

        @pl.when(i >= split_tile)
        def _():
            out_refs[1][...] = out


def _cast_split(n_steps, n_lead, dims):
    split = 1
    while n_lead * split * 2 <= n_steps and all(r % (split * 2 * SLOT_ALIGN) == 0 for r in dims):
        split *= 2
    return split


def _combine_ln(ye, slots_t, gates_t, meta, x2d, g, beta, alpha, split_rows=None, cast=(), cast_layer=0):
    n, d = x2d.shape
    n_tiles, ne, tt = slots_t.shape
    cap_total = ye.shape[1]
    split_tile = None if split_rows is None else split_rows // tt
    split = _cast_split(n_tiles, ne, [w.shape[1] for w in cast]) if cast else 1
    cast_steps = ne * split
    assert not cast or cast_steps <= n_tiles
    kern = functools.partial(_combine_kernel, n_tiles=n_tiles, n_experts=ne, cap_total=cap_total, alpha=alpha,
                             split_tile=split_tile, n_cast=len(cast), cast_steps=cast_steps)

    def cast_spec(w, lead):
        def imap(i, meta):
            s = jnp.minimum(i, cast_steps - 1)
            return (lead + s // split, s % split, 0)
        return pl.BlockSpec((1, w.shape[1] // split, w.shape[2]), imap)

    def win_spec(e):
        def imap(i, meta):
            return (e, pl.multiple_of(meta[(2 * n_tiles + i) * ne + e], SLOT_ALIGN), 0)
        return pl.BlockSpec((pl.Element(1), pl.Element(SLOT_WIN), pl.Element(d)), imap)

    in_specs = [win_spec(e) for e in range(ne)]
    in_specs += [
        pl.BlockSpec((1, ne, tt), lambda i, b: (i, 0, 0)),
        pl.BlockSpec((1, ne, tt), lambda i, b: (i, 0, 0)),
        pl.BlockSpec((tt, d), lambda i, b: (i, 0)),
        pl.BlockSpec((1, d), lambda i, b: (0, 0)),
        pl.BlockSpec((1, d), lambda i, b: (0, 0)),
        pl.BlockSpec(memory_space=pl.ANY),
    ]
    in_specs += [cast_spec(w, cast_layer * ne) for w in cast]
    if split_tile is None:
        out_specs = [pl.BlockSpec((tt, d), lambda i, b: (i, 0))]
        out_shape = [jax.ShapeDtypeStruct((n, d), F32)]
    else:
        out_specs = [pl.BlockSpec((tt, d), lambda i, b: (jnp.minimum(i, split_tile - 1), 0)),
                     pl.BlockSpec((tt, d), lambda i, b: (jnp.maximum(i - split_tile, 0), 0))]
        out_shape = [jax.ShapeDtypeStruct((split_rows, d), F32),
                     jax.ShapeDtypeStruct((n - split_rows, d), F32)]
    out_specs += [cast_spec(w, 0) for w in cast]
    out_shape += [jax.ShapeDtypeStruct((ne,) + w.shape[1:], BF16) for w in cast]
    return pl.pallas_call(
        kern,
        grid_spec=pltpu.PrefetchScalarGridSpec(
            num_scalar_prefetch=1,
            grid=(n_tiles,),
            in_specs=in_specs,
            out_specs=out_specs,
            scratch_shapes=[
                pltpu.VMEM((ne * SLOT_WIN, d), BF16),
                pltpu.SemaphoreType.DMA((1,)),
            ],
        ),
        out_shape=out_shape,
        compiler_params=_params(("arbitrary",)),
        name="moe_combine_ln",
    )(meta, *([ye] * ne), slots_t, gates_t, x2d, g, beta, ye, *cast)


def _filter_kernel(w1t_ref, w1c_ref, w1s_ref, b1_ref, fq_ref, w2_ref, b2_ref, w3_ref, fr_ref, dl_ref,
                   o_ref, *, seq_len):
    half = pl.program_id(0)
    pos = lax.broadcasted_iota(I32, (1, seq_len), 1)
    lag = jnp.where(half == 0, seq_len - pos, pos)
    lagf = lag.astype(F32)
    t = lagf / (seq_len - 1.0)
    w = (2.0 * math.pi) * lagf / seq_len
    fw = fr_ref[...] * w
    hp = lax.Precision.HIGHEST
    pre = (w1t_ref[...] * t
           + jnp.dot(w1c_ref[...], jnp.cos(fw), precision=hp, preferred_element_type=F32)
           - jnp.dot(w1s_ref[...], jnp.sin(fw), precision=hp, preferred_element_type=F32)
           + b1_ref[...])
    fq = fq_ref[...]
    h = jnp.sin(fq * pre)
    h = jnp.sin(fq * (jnp.dot(w2_ref[...], h, precision=hp, preferred_element_type=F32) + b2_ref[...]))
    out = jnp.dot(w3_ref[0], h, precision=hp, preferred_element_type=F32)
    out = out * jnp.exp(-t * dl_ref[...])
    o_ref[...] = jnp.where(lag < seq_len, out, 0.0)


def _hyena_filter(f_w1, f_b1, f_freq, f_w2, f_b2, f_w3, seq_len, d):
    hid = f_w1.shape[1]
    bands = (FILTER_EMB - 1) // 2
    w1 = f_w1.astype(F32).T
    col = lambda v: v.astype(F32).reshape(-1, 1)
    freqs = jnp.linspace(1e-4, bands - 1, bands, dtype=F32).reshape(bands, 1)
    min_decay = math.log(DECAY_TARGET) / SLOW_DECAY_PCT
    max_decay = math.log(DECAY_TARGET) / FAST_DECAY_PCT
    deltas = jnp.abs(jnp.linspace(min_decay, max_decay, d, dtype=F32)).reshape(d, 1)
    w3 = f_w3.astype(F32).T.reshape(2, d, hid)
    full = lambda shape: pl.BlockSpec(shape, lambda s: (0,) * len(shape))
    return pl.pallas_call(
        functools.partial(_filter_kernel, seq_len=seq_len),
        grid=(2,),
        in_specs=[
            full((hid, 1)), full((hid, bands)), full((hid, bands)), full((hid, 1)), full((hid, 1)),
            full((hid, hid)), full((hid, 1)),
            pl.BlockSpec((1, d, hid), lambda s: (1 - s, 0, 0)),
            full((bands, 1)), full((d, 1)),
        ],
        out_specs=pl.BlockSpec((d, seq_len), lambda s: (0, s)),
        out_shape=jax.ShapeDtypeStruct((d, 2 * seq_len), F32),
        compiler_params=_params(("arbitrary",)),
        name="hyena_filter",
    )(w1[:, :1], w1[:, 1:1 + bands], w1[:, 1 + bands:], col(f_b1), col(f_freq),
      f_w2.astype(F32).T, col(f_b2), w3, freqs, deltas)


def _hyena_in_kernel(x_ref, w0_ref, w1_ref, w2_ref, b_ref, cw_ref, cb_ref, x0_ref, z_ref, xb_ref):
    c = pl.program_id(1)

    @pl.when(c == 0)
    def _():
        xb_ref[...] = x_ref[0].astype(BF16)

    xb = xb_ref[...]
    seq_len = xb.shape[0]
    row = lax.broadcasted_iota(I32, (seq_len, 1), 0)
    not_first = row > 0
    not_last = row < seq_len - 1

    def branch(w_ref, k):
        u = jnp.dot(xb, w_ref[...], preferred_element_type=F32) + b_ref[k]
        prev = jnp.where(not_first, pltpu.roll(u, 1, axis=0), 0.0)
        nxt = jnp.where(not_last, pltpu.roll(u, seq_len - 1, axis=0), 0.0)
        cw = cw_ref[k]
        return prev * cw[0:1] + u * cw[1:2] + nxt * cw[2:3] + cb_ref[k]

    x0_ref[0] = branch(w0_ref, 0).T
    z_ref[0] = (branch(w2_ref, 2) * branch(w1_ref, 1)).T


def _hyena_in(x3d, w_in, b_in, conv_w, conv_b):
    b, l, d = x3d.shape
    ch = HY_CH
    nc = d // ch
    b3 = b_in.astype(F32).reshape(3, 1, d)
    cw3 = conv_w.astype(F32).reshape(conv_w.shape[0], 3, d).transpose(1, 0, 2)
    cb3 = conv_b.astype(F32).reshape(3, 1, d)
    wspec = lambda k: pl.BlockSpec((d, ch), lambda i, c, k=k: (0, k * nc + c))
    return pl.pallas_call(
        _hyena_in_kernel,
        grid=(b, nc),
        in_specs=[
            pl.BlockSpec((1, l, d), lambda i, c: (i, 0, 0)),
            wspec(0), wspec(1), wspec(2),
            pl.BlockSpec((3, 1, ch), lambda i, c: (0, 0, c)),
            pl.BlockSpec((3, 3, ch), lambda i, c: (0, 0, c)),
            pl.BlockSpec((3, 1, ch), lambda i, c: (0, 0, c)),
        ],
        out_specs=[
            pl.BlockSpec((1, ch, l), lambda i, c: (i, c, 0)),
            pl.BlockSpec((1, ch, l), lambda i, c: (i, c, 0)),
        ],
        out_shape=[
            jax.ShapeDtypeStruct((b, d, l), F32),
            jax.ShapeDtypeStruct((b, d, l), F32),
        ],
        scratch_shapes=[pltpu.VMEM((l, d), BF16)],
        compiler_params=_params(("parallel", "arbitrary")),
        name="hyena_in_conv",
    )(x3d, w_in, w_in, w_in, b3, cw3, cb3)


def _long_conv_kernel(z_ref, x0_ref, kk_ref, skip_ref, o_ref, *, n_blk):
    blk = CONV_BLOCK
    nb = z_ref.shape[0]

    for c in range(z_ref.shape[1]):
        zc = z_ref[:, c, :]
        zr = jnp.concatenate([zc[:, j * blk:(j + 1) * blk] for j in range(n_blk)], axis=0).astype(BF16)
        kk = kk_ref[c:c + 1, :]
        acc = [None] * n_blk
        for delta in range(-(n_blk - 1), n_blk):
            start = (n_blk + delta - 1) * blk
            wrow = jnp.broadcast_to(kk[:, start:start + 2 * blk], (blk, 2 * blk))
            rolled = pltpu.roll(wrow, 0, axis=1, stride=1, stride_axis=0)
            tt = rolled[:, blk:].astype(BF16)
            j0, j1 = max(0, -delta), min(n_blk, n_blk - delta)
            res = jnp.dot(zr[j0 * nb:j1 * nb, :], tt, preferred_element_type=F32)
            for j in range(j0, j1):
                part = res[(j - j0) * nb:(j - j0 + 1) * nb, :]
                i = j + delta
                acc[i] = part if acc[i] is None else acc[i] + part
        y = jnp.concatenate(acc, axis=1)
        o_ref[:, c, :] = (y + zc * skip_ref[c:c + 1, :]) * x0_ref[:, c, :]


def _long_conv(z_t, x0_t, kk, skip):
    b, d, l = z_t.shape
    ch = CONV_CH
    kern = functools.partial(_long_conv_kernel, n_blk=l // CONV_BLOCK)
    return pl.pallas_call(
        kern,
        grid=(d // ch,),
        in_specs=[
            pl.BlockSpec((b, ch, l), lambda i: (0, i, 0)),
            pl.BlockSpec((b, ch, l), lambda i: (0, i, 0)),
            pl.BlockSpec((ch, 2 * l), lambda i: (i, 0)),
            pl.BlockSpec((ch, 1), lambda i: (i, 0)),
        ],
        out_specs=pl.BlockSpec((b, ch, l), lambda i: (0, i, 0)),
        out_shape=jax.ShapeDtypeStruct((b, d, l), F32),
        compiler_params=_params(("parallel",)),
        name="hyena_long_conv",
    )(z_t, x0_t, kk, skip)


def _rope_tables(seq_len, head_dim):
    rows = seq_len // GRID_W
    row = jnp.repeat(jnp.arange(rows), GRID_W)
    col = jnp.tile(jnp.arange(GRID_W), rows)
    axis_dim = head_dim // 2
    inv = ROPE_THETA ** (-jnp.arange(0, axis_dim, 2, dtype=F32) / axis_dim)
    ang = jnp.concatenate([row[:, None] * inv, col[:, None] * inv], -1)
    cos, sin = jnp.cos(ang), jnp.sin(ang)
    reps = LANES // head_dim
    cc = jnp.tile(jnp.concatenate([cos, cos], -1), (1, reps))
    ss = jnp.tile(jnp.concatenate([-sin, sin], -1), (1, reps))
    return cc, ss


def _moe(x1, aff, group_tokens, w_gate, w_up, w_down, layer, g, beta, alpha, split_rows=None, cast=(),
         cast_layer=0):
    n, d = x1.shape
    ne = aff.shape[1]
    tt = TOK_TILE
    slots, gates, bases = [], [], []
    tok0, slot0 = 0, 0
    for ng in group_tokens:
        cap = EC_CAPACITY * ng // ne
        s, gt, bs = _select(aff[tok0:tok0 + ng].T.reshape(ne, ng // tt, tt), cap, slot0)
        slots.append(s)
        gates.append(gt)
        bases.append(bs[:, :, 0])
        tok0 += ng
        slot0 += cap
    cap_total = slot0
    slots_t = jnp.concatenate(slots, axis=1).transpose(1, 0, 2)
    gates_t = jnp.concatenate(gates, axis=1).transpose(1, 0, 2)
    base = jnp.concatenate(bases + [jnp.full((ne, 1), cap_total, I32)], axis=1).T
    meta = _routing_meta(base, cap_total)
    xe = _dispatch(x1, slots_t, meta, cap_total)
    ye = _expert_ffn(xe, w_gate, w_up, w_down, cap_total, layer)
    return _combine_ln(ye, slots_t, gates_t, meta, x1, g, beta, alpha, split_rows, cast, cast_layer)


def kernel(x_prompt, x_sample, attn_w_qkv, attn_q_gain, attn_k_gain, attn_w_o, hy_w_in, hy_b_in, hy_conv_w, hy_conv_b, hy_f_w1, hy_f_b1, hy_f_freq, hy_f_w2, hy_f_b2, hy_f_w3, hy_skip, hy_w_out, hy_b_out, ln_mix_g, ln_mix_b, moe_router, moe_w_gate, moe_w_up, moe_w_down, ln_ffn_g, ln_ffn_b):
    bp, l, d = x_prompt.shape
    bs = x_sample.shape[0]
    assert x_sample.shape[1] == l
    depth = ln_mix_g.shape[0]
    alpha = (2 * depth) ** 0.25
    head_dim = d // N_HEADS
    b = bp + bs
    n = b * l
    group_tokens = (bp * l, bs * l)

    xs = [x_prompt.reshape(bp * l, d), x_sample.reshape(bs * l, d)]
    cc, ss = _rope_tables(l, head_dim)
    seg = (lax.broadcasted_iota(I32, (MXU_DIM, MXU_DIM), 0) // head_dim
           == lax.broadcasted_iota(I32, (MXU_DIM, MXU_DIM), 1) // head_dim).astype(BF16)
    row = lambda v: v.astype(F32).reshape(1, -1)
    zero_bias = jnp.zeros((1, d), F32)
    moe_w = (moe_w_gate, moe_w_up, moe_w_down)
    w_bf16 = tuple(w[0].astype(BF16) for w in moe_w)
    moe_w_stacked = tuple(w.reshape((-1,) + w.shape[2:]) for w in moe_w)
    cast_in_kernel = l * b // TOK_TILE >= moe_w_gate.shape[1]

    for i in range(depth):
        j = i // 2
        r_hi, r_lo = _split_bf16(moe_router[i].astype(F32))
        if i % 2 == 0:
            gq = jnp.tile(row(attn_q_gain[j]), (1, LANES // head_dim))
            gk = jnp.tile(row(attn_k_gain[j]), (1, LANES // head_dim))
            q, k, v = _qkv_rope(xs, attn_w_qkv[j].astype(BF16), cc, ss, gq, gk, seg, l)
            o = _attention(q.reshape(b, l, -1), k.reshape(b, l, -1), v.reshape(b, l, -1), tq=min(ATTN_STEP, l))
            mix_in, w_mix, b_mix = o.reshape(n, -1), attn_w_o[j].astype(BF16), zero_bias
        else:
            kk = _hyena_filter(hy_f_w1[j], hy_f_b1[j], hy_f_freq[j], hy_f_w2[j], hy_f_b2[j], hy_f_w3[j], l, d)
            (x,) = xs
            x0, z = _hyena_in(x.reshape(b, l, d), hy_w_in[j].astype(BF16), hy_b_in[j], hy_conv_w[j], hy_conv_b[j])
            mix_in = _long_conv(z, x0, kk, hy_skip[j].astype(F32).reshape(d, 1))
            w_mix, b_mix = hy_w_out[j].astype(BF16), row(hy_b_out[j])
        x1, aff = _proj_ln_router(mix_in, w_mix, b_mix, xs, row(ln_mix_g[i]), row(ln_mix_b[i]),
                                  r_hi, r_lo, alpha)
        last = i == depth - 1
        cast = moe_w_stacked if cast_in_kernel and not last else ()
        out = _moe(x1, aff, group_tokens, *w_bf16, 0, row(ln_ffn_g[i]), row(ln_ffn_b[i]), alpha,
                   split_rows=group_tokens[0] if last else None, cast=cast, cast_layer=i + 1)
        if last:
            return (out[0].reshape(bp, l, d), out[1].reshape(bs, l, d))
        xs = [out[0]]
        w_bf16 = tuple(out[1:]) if cast else tuple(w[i + 1].astype(BF16) for w in moe_w)
```
